```python
import jax, jax.numpy as jnp
from jax import lax
import numpy as np

D_MODEL = 1024
BATCH = 2
SEQ = 8192
DEPTH = 2

CTX_LEN = 256
GRID_W = 64
D_MIX = D_MODEL
RET_DIM = D_MIX // 2
DIFF_DIM = D_MIX - RET_DIM
RET_HEADS = 4
RET_HEAD_DIM = RET_DIM // RET_HEADS
DIFF_HEADS = 4
DIFF_V_HEAD = DIFF_DIM // DIFF_HEADS
DIFF_QK_HEAD = DIFF_V_HEAD // 2
D_PROJ = 4 * RET_DIM + 3 * DIFF_DIM
PROJ_SPLITS = [RET_DIM, 2 * RET_DIM, 3 * RET_DIM, 4 * RET_DIM, 4 * RET_DIM + DIFF_DIM, 4 * RET_DIM + 2 * DIFF_DIM]
CHUNK = 128
Q_BLOCK = 128
D_FF = 2816
N_EXPERTS = 8
TOP_K = 2
N_DENSE = (DEPTH + 1) // 2
N_MOE = DEPTH // 2
ROPE_BASE = 10000.0
EPS = 1e-6

kernel_name = "hymba_retnet_diffattn_moe_prefix_block"


def rms_norm(x, g):
    xf = x.astype(jnp.float32)
    y = xf * lax.rsqrt(jnp.mean(xf * xf, axis=-1, keepdims=True) + EPS)
    return (y * g.astype(jnp.float32)).astype(x.dtype)


def modulate(h, shift, scale):
    return h * (1.0 + scale) + shift


def axial_rope(row_pos, col_pos, dim):
    n_freq = dim // 4
    inv_freq = ROPE_BASE ** (-jnp.arange(n_freq, dtype=jnp.float32) / n_freq)
    ang = jnp.concatenate([row_pos[:, None] * inv_freq, col_pos[:, None] * inv_freq], axis=-1)
    ang = jnp.concatenate([ang, ang], axis=-1)
    return jnp.cos(ang), jnp.sin(ang)


def apply_rope(x, cos, sin):
    half = x.shape[-1] // 2
    xf = x.astype(jnp.float32)
    rot = jnp.concatenate([-xf[..., half:], xf[..., :half]], axis=-1)
    return (xf * cos + rot * sin).astype(x.dtype)


def to_heads(a, n_heads):
    b, t, _ = a.shape
    return a.reshape(b, t, n_heads, -1).transpose(0, 2, 1, 3)


def retention_chunkwise(q, k, v, log_gamma, s0):
    b, h, t, dk = q.shape
    dv = v.shape[-1]
    n = t // CHUNK
    qc = q.astype(jnp.float32).reshape(b, h, n, CHUNK, dk)
    kc = k.astype(jnp.float32).reshape(b, h, n, CHUNK, dk)
    vc = v.astype(jnp.float32).reshape(b, h, n, CHUNK, dv)
    pos = jnp.arange(CHUNK, dtype=jnp.float32)
    rel = pos[:, None] - pos[None, :]
    dmat = jnp.where(rel >= 0, jnp.exp(log_gamma[:, None, None] * jnp.maximum(rel, 0.0)), 0.0)
    scores = jnp.einsum('bhncd,bhnmd->bhncm', qc, kc) * dmat[None, :, None]
    o_inner = jnp.einsum('bhncm,bhnme->bhnce', scores, vc)
    lg = log_gamma[:, None]
    zeta = jnp.exp(lg * (CHUNK - 1.0 - pos))
    xi = jnp.exp(lg * (pos + 1.0))
    kv = jnp.einsum('bhnmd,bhnme->nbhde', kc * zeta[None, :, None, :, None], vc)
    chunk_decay = jnp.exp(log_gamma * CHUNK)[None, :, None, None]

    def step(s, kv_n):
        return chunk_decay * s + kv_n, s

    s_final, s_prev = lax.scan(step, s0.astype(jnp.float32), kv)
    o_cross = jnp.einsum('bhncd,nbhde->bhnce', qc * xi[None, :, None, :, None], s_prev)
    return (o_inner + o_cross).reshape(b, h, t, dv), s_final


def retention_final_state(k, v, log_gamma):
    t = k.shape[2]
    w = jnp.exp(log_gamma[:, None] * (t - 1.0 - jnp.arange(t, dtype=jnp.float32)))
    return jnp.einsum('bhtd,bhte->bhde', k.astype(jnp.float32) * w[None, :, :, None], v.astype(jnp.float32))


def retention_out(y, gproj, gn_g):
    b, h, t, dv = y.shape
    mu = jnp.mean(y, axis=-1, keepdims=True)
    yc = y - mu
    yn = yc * lax.rsqrt(jnp.mean(yc * yc, axis=-1, keepdims=True) + EPS)
    yn = yn.transpose(0, 2, 1, 3).reshape(b, t, h * dv) * gn_g.astype(jnp.float32)
    return (jax.nn.silu(gproj.astype(jnp.float32)) * yn).astype(gproj.dtype)


def diff_qk(p, g):
    b, t, _ = p.shape
    a = p.reshape(b, t, DIFF_HEADS, 2, DIFF_QK_HEAD).transpose(0, 2, 3, 1, 4)
    return rms_norm(a, g)


def diff_attend(q, k, v, lam):
    b, h, _, tq, d = q.shape
    dv = v.shape[-1]
    nb = tq // Q_BLOCK
    qb = jnp.moveaxis(q.reshape(b, h, 2, nb, Q_BLOCK, d), 3, 0)

    def block(qi):
        s = jnp.einsum('bhcqd,bhckd->bhcqk', qi, k).astype(jnp.float32)
        p = jax.nn.softmax(s, axis=-1)
        a = p[:, :, 0] - lam * p[:, :, 1]
        return jnp.einsum('bhqk,bhkv->bhqv', a.astype(v.dtype), v)

    o = lax.map(block, qb)
    return jnp.moveaxis(o, 0, 2).reshape(b, h, tq, dv)


def diff_post(o, g, lam_init):
    b, h, t, dv = o.shape
    y = rms_norm(o, g) * (1.0 - lam_init)
    return y.transpose(0, 2, 1, 3).reshape(b, t, h * dv)


def hybrid_mixer(h, hc, w_in, dec_f, dec_b, gn_g, qn_g, kn_g, lq1, lk1, lq2, lk2, subln_g, w_out,
                 lam_init, cos, sin, ctx_out):
    rq, rk, rv, rg, dq, dk, dv = jnp.split(h @ w_in, PROJ_SPLITS, axis=-1)
    if ctx_out:
        rqc, rkc, rvc, rgc, dqc, dkc, dvc = jnp.split(hc @ w_in, PROJ_SPLITS, axis=-1)
    else:
        rkc, rvc = jnp.split(hc @ w_in[:, RET_DIM:3 * RET_DIM], 2, axis=-1)
        dkc, dvc = jnp.split(hc @ w_in[:, 4 * RET_DIM + DIFF_DIM:], 2, axis=-1)

    lg_f = jax.nn.log_sigmoid(dec_f.astype(jnp.float32))
    lg_b = jax.nn.log_sigmoid(dec_b.astype(jnp.float32))
    k_scale = RET_HEAD_DIM ** -0.5
    flip = lambda a: jnp.flip(a, axis=2)
    ql, kl, vl = to_heads(rq, RET_HEADS), to_heads(rk, RET_HEADS) * k_scale, to_heads(rv, RET_HEADS)
    kc_r, vc_r = to_heads(rkc, RET_HEADS) * k_scale, to_heads(rvc, RET_HEADS)
    if ctx_out:
        qc_r = to_heads(rqc, RET_HEADS)
        zeros = jnp.zeros((hc.shape[0], RET_HEADS, RET_HEAD_DIM, RET_HEAD_DIM), jnp.float32)
        oc_f, sc_f = retention_chunkwise(qc_r, kc_r, vc_r, lg_f, zeros)
        oc_b, sc_b = retention_chunkwise(flip(qc_r), flip(kc_r), flip(vc_r), lg_b, zeros)
        ret_c = retention_out(oc_f + flip(oc_b), rgc, gn_g)
    else:
        sc_f = retention_final_state(kc_r, vc_r, lg_f)
        sc_b = retention_final_state(flip(kc_r), flip(vc_r), lg_b)
    o_f, _ = retention_chunkwise(ql, kl, vl, lg_f, sc_f)
    o_b, _ = retention_chunkwise(flip(ql), flip(kl), flip(vl), lg_b, sc_b)
    ret = retention_out(o_f + flip(o_b), rg, gn_g)

    f32 = jnp.float32
    lam = (jnp.exp(jnp.sum(lq1.astype(f32) * lk1.astype(f32)))
           - jnp.exp(jnp.sum(lq2.astype(f32) * lk2.astype(f32))) + lam_init)
    q_scale = DIFF_QK_HEAD ** -0.5
    qd = apply_rope(diff_qk(dq, qn_g), cos, sin) * q_scale
    kd = apply_rope(diff_qk(dk, kn_g), cos, sin)
    vd = to_heads(dv, DIFF_HEADS)
    kdc = diff_qk(dkc, kn_g)
    vdc = to_heads(dvc, DIFF_HEADS)
    k_all = jnp.concatenate([kdc, kd], axis=3)
    v_all = jnp.concatenate([vdc, vd], axis=2)
    diff = diff_post(diff_attend(qd, k_all, v_all, lam), subln_g, lam_init)
    out = jnp.concatenate([ret, diff], axis=-1) @ w_out
    if ctx_out:
        qdc = diff_qk(dqc, qn_g) * q_scale
        diff_c = diff_post(diff_attend(qdc, kdc, vdc, lam), subln_g, lam_init)
        out_c = jnp.concatenate([ret_c, diff_c], axis=-1) @ w_out
        return out, out_c
    return out, None


def swiglu(h, wg, wu, wd):
    return (jax.nn.silu(h @ wg) * (h @ wu)) @ wd


def moe_swiglu(h, rw, rb, wg, wu, wd):
    shp = h.shape
    t = h.reshape(-1, shp[-1])
    logits = (t @ rw).astype(jnp.float32) + rb.astype(jnp.float32)
    top_v, top_i = lax.top_k(logits, TOP_K)
    top_w = jax.nn.softmax(top_v, axis=-1)
    gates = jnp.sum(jax.nn.one_hot(top_i, N_EXPERTS, dtype=jnp.float32) * top_w[..., None], axis=1)
    y = jnp.zeros_like(t)
    for e in range(N_EXPERTS):
        y = y + gates[:, e:e + 1].astype(t.dtype) * swiglu(t, wg[e], wu[e], wd[e])
    return y.reshape(shp)


def setup_inputs(seed: int = 0) -> dict:
    key = jax.random.key(seed)
    ks = jax.random.split(key, 28)
    nrm = jax.random.normal
    f32 = jnp.float32
    base_decay = jnp.log(2.0 ** (5.0 + jnp.arange(RET_HEADS, dtype=f32)) - 1.0)
    return {
        "x": nrm(ks[0], (BATCH, SEQ, D_MODEL), f32),
        "c": nrm(ks[1], (BATCH, D_MODEL), f32),
        "ctx": nrm(ks[2], (BATCH, CTX_LEN, D_MODEL), f32),
        "c_ctx": nrm(ks[3], (D_MODEL,), f32),
        "ada_w": nrm(ks[4], (DEPTH, D_MODEL, 6 * D_MODEL), f32) * D_MODEL ** -0.5,
        "ada_b": nrm(ks[5], (DEPTH, 6 * D_MODEL), f32) * 0.02,
        "norm1_g": 1.0 + 0.02 * nrm(ks[6], (DEPTH, D_MODEL), f32),
        "norm2_g": 1.0 + 0.02 * nrm(ks[7], (DEPTH, D_MODEL), f32),
        "w_in": nrm(ks[8], (DEPTH, D_MODEL, D_PROJ), f32) * D_MODEL ** -0.5,
        "ret_decay_fwd": base_decay + 0.05 * nrm(ks[9], (DEPTH, RET_HEADS), f32),
        "ret_decay_bwd": base_decay + 0.05 * nrm(ks[10], (DEPTH, RET_HEADS), f32),
        "ret_gn_g": 1.0 + 0.02 * nrm(ks[11], (DEPTH, RET_DIM), f32),
        "diff_qn_g": 1.0 + 0.02 * nrm(ks[12], (DEPTH, DIFF_QK_HEAD), f32),
        "diff_kn_g": 1.0 + 0.02 * nrm(ks[13], (DEPTH, DIFF_QK_HEAD), f32),
        "lam_q1": 0.1 * nrm(ks[14], (DEPTH, DIFF_QK_HEAD), f32),
        "lam_k1": 0.1 * nrm(ks[15], (DEPTH, DIFF_QK_HEAD), f32),
        "lam_q2": 0.1 * nrm(ks[16], (DEPTH, DIFF_QK_HEAD), f32),
        "lam_k2": 0.1 * nrm(ks[17], (DEPTH, DIFF_QK_HEAD), f32),
        "diff_subln_g": 1.0 + 0.02 * nrm(ks[18], (DEPTH, DIFF_V_HEAD), f32),
        "w_out": nrm(ks[19], (DEPTH, D_MIX, D_MODEL), f32) * D_MIX ** -0.5,
        "ffn_w_gate": nrm(ks[20], (N_DENSE, D_MODEL, D_FF), f32) * D_MODEL ** -0.5,
        "ffn_w_up": nrm(ks[21], (N_DENSE, D_MODEL, D_FF), f32) * D_MODEL ** -0.5,
        "ffn_w_down": nrm(ks[22], (N_DENSE, D_FF, D_MODEL), f32) * D_FF ** -0.5,
        "router_w": nrm(ks[23], (N_MOE, D_MODEL, N_EXPERTS), f32) * D_MODEL ** -0.5,
        "router_b": nrm(ks[24], (N_MOE, N_EXPERTS), f32) * 0.01,
        "moe_w_gate": nrm(ks[25], (N_MOE, N_EXPERTS, D_MODEL, D_FF), f32) * D_MODEL ** -0.5,
        "moe_w_up": nrm(ks[26], (N_MOE, N_EXPERTS, D_MODEL, D_FF), f32) * D_MODEL ** -0.5,
        "moe_w_down": nrm(ks[27], (N_MOE, N_EXPERTS, D_FF, D_MODEL), f32) * D_FF ** -0.5,
    }


def reference(x, c, ctx, c_ctx, ada_w, ada_b, norm1_g, norm2_g, w_in, ret_decay_fwd, ret_decay_bwd,
              ret_gn_g, diff_qn_g, diff_kn_g, lam_q1, lam_k1, lam_q2, lam_k2, diff_subln_g, w_out,
              ffn_w_gate, ffn_w_up, ffn_w_down, router_w, router_b, moe_w_gate, moe_w_up, moe_w_down):
    b, t, _ = x.shape
    rows = t // GRID_W
    row_pos = jnp.repeat(jnp.arange(rows, dtype=jnp.float32), GRID_W)
    col_pos = jnp.tile(jnp.arange(GRID_W, dtype=jnp.float32), rows)
    cos, sin = axial_rope(row_pos, col_pos, DIFF_QK_HEAD)

    def channel_mixer(h, l):
        if l % 2 == 0:
            i = l // 2
            return swiglu(h, ffn_w_gate[i], ffn_w_up[i], ffn_w_down[i])
        i = l // 2
        return moe_swiglu(h, router_w[i], router_b[i], moe_w_gate[i], moe_w_up[i], moe_w_down[i])

    xc = ctx
    for l in range(DEPTH):
        ctx_out = l < DEPTH - 1
        lam_init = 0.8 - 0.6 * float(np.exp(-0.3 * l))
        mod = jnp.split(jax.nn.silu(c) @ ada_w[l] + ada_b[l], 6, axis=-1)
        mod = [m[:, None, :] for m in mod]
        modc = jnp.split(jax.nn.silu(c_ctx) @ ada_w[l] + ada_b[l], 6, axis=-1)

        h = modulate(rms_norm(x, norm1_g[l]), mod[0], mod[1])
        hc = modulate(rms_norm(xc, norm1_g[l]), modc[0], modc[1])
        mix, mix_c = hybrid_mixer(h, hc, w_in[l], ret_decay_fwd[l], ret_decay_bwd[l], ret_gn_g[l],
                                  diff_qn_g[l], diff_kn_g[l], lam_q1[l], lam_k1[l], lam_q2[l], lam_k2[l],
                                  diff_subln_g[l], w_out[l], lam_init, cos, sin, ctx_out)
        x = x + mod[2] * mix
        h2 = modulate(rms_norm(x, norm2_g[l]), mod[3], mod[4])
        x = x + mod[5] * channel_mixer(h2, l)
        if ctx_out:
            xc = xc + modc[2] * mix_c
            h2c = modulate(rms_norm(xc, norm2_g[l]), modc[3], modc[4])
            xc = xc + modc[5] * channel_mixer(h2c, l)
    return x
```

```python
import functools
import math

import jax
import jax.numpy as jnp
from jax import lax
from jax.experimental import pallas as pl
from jax.experimental.pallas import tpu as pltpu

F32 = jnp.float32
BF16 = jnp.bfloat16
HIGHEST = lax.Precision.HIGHEST

EPS = 1e-6
ROPE_BASE = 10000.0
GRID_W = 64
RET_HEADS = 4
DIFF_HEADS = 4
HEAD_W = 128
QK_HEAD = 64
CHUNK = 128
TOP_K = 2
LANES = 128
SUBLANES = 8
VMEM_LIMIT = 56 * 1024 * 1024


def _cparams(*sem):
    return pltpu.CompilerParams(dimension_semantics=sem, vmem_limit_bytes=VMEM_LIMIT)


def _silu(x):
    return x * (1.0 / (1.0 + jnp.exp(-x)))


def _pick(n, cands):
    for c in cands:
        if n % c == 0:
            return c
    raise ValueError(f"no tile for {n}")


def _ada_kernel(c_ref, w_ref, b_ref, o_ref):
    s = _silu(c_ref[...])
    o_ref[0] = jnp.dot(s, w_ref[0], preferred_element_type=F32, precision=HIGHEST) + b_ref[0]


def _ada(c8, ada_w, ada_b):
    depth, d, n = ada_w.shape
    tn = _pick(n, (1536, 1024, 512, 256, 128))
    return pl.pallas_call(
        _ada_kernel,
        grid=(depth, n // tn),
        in_specs=[pl.BlockSpec((SUBLANES, d), lambda l, j: (0, 0)),
                  pl.BlockSpec((1, d, tn), lambda l, j: (l, 0, j)),
                  pl.BlockSpec((1, 1, tn), lambda l, j: (l, 0, j))],
        out_specs=pl.BlockSpec((1, SUBLANES, tn), lambda l, j: (l, 0, j)),
        out_shape=jax.ShapeDtypeStruct((depth, SUBLANES, n), F32),
        compiler_params=_cparams("parallel", "parallel"),
        name="ada",
    )(c8, ada_w, ada_b.reshape(depth, 1, n))


def _mod_rows(mod_ref, r, d, ks):
    return [mod_ref[pl.ds(r, 1), k * d:(k + 1) * d] for k in ks]


def _norm_mod(x, g, shift, scale):
    ms = jnp.sum(x * x, axis=-1, keepdims=True) * (1.0 / x.shape[-1])
    y = x * lax.rsqrt(ms + EPS) * g
    return y * (1.0 + scale) + shift


def _inproj_kernel(x_ref, mod_ref, g_ref, w_ref, qg_ref, kg_ref, cos_ref, sin_ref,
                   rq_ref, rk_ref, rv_ref, rg_ref, dq_ref, dk1_ref, dk2_ref, dv_ref,
                   *, row0, tpb, d):
    i = pl.program_id(0)
    r = row0 + i // tpb
    shift, scale = _mod_rows(mod_ref, r, d, (0, 1))
    h = _norm_mod(x_ref[...], g_ref[...], shift, scale).astype(BF16)
    w = 4 * HEAD_W

    def proj(j):
        return jnp.dot(h, w_ref[:, j * w:(j + 1) * w], preferred_element_type=F32)

    rq_ref[...] = proj(0).astype(BF16)
    rk_ref[...] = (proj(1) * (HEAD_W ** -0.5)).astype(BF16)
    rv_ref[...] = proj(2).astype(BF16)
    rg_ref[...] = proj(3).astype(BF16)

    tm = h.shape[0]
    lane = lax.broadcasted_iota(jnp.int32, (tm, w), 1)
    rowi = lax.broadcasted_iota(jnp.int32, (w, w), 0) // QK_HEAD
    coli = lax.broadcasted_iota(jnp.int32, (w, w), 1) // QK_HEAD
    seg = jnp.where(rowi == coli, 1.0, 0.0).astype(BF16)
    low = (lane % QK_HEAD) < (QK_HEAD // 2)
    cos = jnp.concatenate([cos_ref[...]] * (w // LANES), axis=1)
    sin = jnp.concatenate([sin_ref[...]] * (w // LANES), axis=1)

    def qk_norm_rope(a, gain):
        sq = a * a
        hi = sq.astype(BF16)
        lo = (sq - hi.astype(F32)).astype(BF16)
        ss = (jnp.dot(hi, seg, preferred_element_type=F32)
              + jnp.dot(lo, seg, preferred_element_type=F32))
        y = a * lax.rsqrt(ss * (1.0 / QK_HEAD) + EPS) * gain
        rot = jnp.where(low, pltpu.roll(y, w - QK_HEAD // 2, 1), pltpu.roll(y, QK_HEAD // 2, 1))
        return y * cos + rot * sin

    q = qk_norm_rope(proj(4), qg_ref[...]) * (QK_HEAD ** -0.5)
    dq_ref[...] = q.astype(BF16)
    k = qk_norm_rope(proj(5), kg_ref[...])
    first = (lane % HEAD_W) < QK_HEAD
    dk1_ref[...] = jnp.where(first, k, 0.0).astype(BF16)
    dk2_ref[...] = jnp.where(first, 0.0, k).astype(BF16)
    dv_ref[...] = proj(6).astype(BF16)


def _inproj(x2, modl, g1, w_bf, qg, kg, cos, sin, *, row0, tpb, tm):
    r, d = x2.shape
    w = 4 * HEAD_W
    n_tiles = r // tm
    row = lambda i: (i, 0)
    full = lambda i: (0, 0)
    tab = lambda i: (i % tpb, 0)
    out = jax.ShapeDtypeStruct((r, w), BF16)
    return pl.pallas_call(
        functools.partial(_inproj_kernel, row0=row0, tpb=tpb, d=d),
        grid=(n_tiles,),
        in_specs=[pl.BlockSpec((tm, d), row),
                  pl.BlockSpec(modl.shape, full),
                  pl.BlockSpec((1, d), full),
                  pl.BlockSpec(w_bf.shape, full),
                  pl.BlockSpec((1, w), full),
                  pl.BlockSpec((1, w), full),
                  pl.BlockSpec((tm, LANES), tab),
                  pl.BlockSpec((tm, LANES), tab)],
        out_specs=[pl.BlockSpec((tm, w), row)] * 8,
        out_shape=[out] * 8,
        compiler_params=_cparams("parallel"),
        name="inproj",
    )(x2, modl, g1, w_bf, qg, kg, cos, sin)


def _ret_kernel(lg_ref, q_ref, k_ref, v_ref, g_ref, gn_ref, s0f_ref, s0b_ref,
                o_ref, sf_ref, sb_ref, of_scr, ob_scr, *, n_chunks):
    hd = pl.program_id(1)
    c = CHUNK
    lgf = jnp.full((c, c), lg_ref[0, hd], F32)
    lgb = jnp.full((c, c), lg_ref[1, hd], F32)
    row = lax.broadcasted_iota(jnp.int32, (c, c), 0).astype(F32)
    col = lax.broadcasted_iota(jnp.int32, (c, c), 1).astype(F32)
    rel = row - col
    dmat_f = jnp.where(rel >= 0, jnp.exp(lgf * jnp.maximum(rel, 0.0)), 0.0)
    dmat_b = jnp.where(rel <= 0, jnp.exp(lgb * jnp.maximum(-rel, 0.0)), 0.0)
    zeta_f = jnp.exp(lgf * (c - 1.0 - row))
    xi_f = jnp.exp(lgf * (row + 1.0))
    zeta_b = jnp.exp(lgb * row)
    xi_b = jnp.exp(lgb * (c - row))
    dec_f = jnp.exp(lgf * c)
    dec_b = jnp.exp(lgb * c)

    sf_ref[0, 0] = s0f_ref[0, 0]
    sb_ref[0, 0] = s0b_ref[0, 0]

    nt = (((1,), (1,)), ((), ()))

    def one(ci, dmat, zeta, xi, dec, s_ref, o_scr):
        rows = pl.ds(pl.multiple_of(ci * c, c), c)
        q = q_ref[rows, :]
        k = k_ref[rows, :]
        v = v_ref[rows, :]
        s = s_ref[0, 0]
        sc = lax.dot_general(q, k, nt, preferred_element_type=F32) * dmat
        o = jnp.dot(sc.astype(BF16), v, preferred_element_type=F32)
        o = o + xi * jnp.dot(q, s.astype(BF16), preferred_element_type=F32)
        o_scr[rows, :] = o
        kzt = (k.astype(F32) * zeta).T.astype(BF16)
        s_ref[0, 0] = dec * s + jnp.dot(kzt, v, preferred_element_type=F32)

    def body(i, carry):
        one(i, dmat_f, zeta_f, xi_f, dec_f, sf_ref, of_scr)
        one(n_chunks - 1 - i, dmat_b, zeta_b, xi_b, dec_b, sb_ref, ob_scr)
        return carry

    lax.fori_loop(0, n_chunks, body, 0)

    gn = gn_ref[...]

    def fin(ci, carry):
        rows = pl.ds(pl.multiple_of(ci * c, c), c)
        y = of_scr[rows, :] + ob_scr[rows, :]
        mu = jnp.sum(y, axis=-1, keepdims=True) * (1.0 / HEAD_W)
        yc = y - mu
        var = jnp.sum(yc * yc, axis=-1, keepdims=True) * (1.0 / HEAD_W)
        yn = yc * lax.rsqrt(var + EPS) * gn
        o_ref[rows, :] = (_silu(g_ref[rows, :].astype(F32)) * yn).astype(BF16)
        return carry

    lax.fori_loop(0, n_chunks, fin, 0)


def _retention(lg, rq, rk, rv, rg, gn, s0f, s0b, *, nb, t):
    n_chunks = t // CHUNK
    blk = pl.BlockSpec((t, HEAD_W), lambda b, h: (b, h))
    sblk = pl.BlockSpec((1, 1, HEAD_W, HEAD_W), lambda b, h: (b, h, 0, 0))
    s_shape = jax.ShapeDtypeStruct((nb, RET_HEADS, HEAD_W, HEAD_W), F32)
    return pl.pallas_call(
        functools.partial(_ret_kernel, n_chunks=n_chunks),
        grid=(nb, RET_HEADS),
        in_specs=[pl.BlockSpec(memory_space=pltpu.SMEM),
                  blk, blk, blk, blk,
                  pl.BlockSpec((1, HEAD_W), lambda b, h: (0, h)),
                  sblk, sblk],
        out_specs=[blk, sblk, sblk],
        out_shape=[jax.ShapeDtypeStruct(rq.shape, BF16), s_shape, s_shape],
        scratch_shapes=[pltpu.VMEM((t, HEAD_W), F32), pltpu.VMEM((t, HEAD_W), F32)],
        compiler_params=_cparams("parallel", "parallel"),
        name="retention",
    )(lg, rq, rk, rv, rg, gn, s0f, s0b)


def _attn_kernel(lam_ref, q_ref, k1_ref, k2_ref, v_ref, g_ref, o_ref, *, tk, n_kv, post):
    q = q_ref[...]
    tq = q.shape[0]
    nt = (((1,), (1,)), ((), ()))

    def comp(k, v, m, l, a):
        s = lax.dot_general(q, k, nt, preferred_element_type=F32)
        mn = jnp.maximum(m, jnp.max(s, axis=-1, keepdims=True))
        al = jnp.exp(m - mn)
        p = jnp.exp(s - mn)
        l = al * l + jnp.sum(p, axis=-1, keepdims=True)
        a = al * a + jnp.dot(p.astype(BF16), v, preferred_element_type=F32)
        return mn, l, a

    def body(j, carry):
        m1, l1, a1, m2, l2, a2 = carry
        rows = pl.ds(pl.multiple_of(j * tk, tk), tk)
        v = v_ref[rows, :]
        m1, l1, a1 = comp(k1_ref[rows, :], v, m1, l1, a1)
        m2, l2, a2 = comp(k2_ref[rows, :], v, m2, l2, a2)
        return m1, l1, a1, m2, l2, a2

    m0 = jnp.full((tq, 1), -1e30, F32)
    l0 = jnp.zeros((tq, 1), F32)
    a0 = jnp.zeros((tq, HEAD_W), F32)
    _, l1, a1, _, l2, a2 = lax.fori_loop(0, n_kv, body, (m0, l0, a0, m0, l0, a0))
    o = a1 * (1.0 / l1) - lam_ref[0] * (a2 * (1.0 / l2))
    ms = jnp.sum(o * o, axis=-1, keepdims=True) * (1.0 / HEAD_W)
    o_ref[...] = (o * lax.rsqrt(ms + EPS) * g_ref[...] * post).astype(BF16)


def _attention(lam, q, k1, k2, v, g, *, nb, tq_total, tk_total, post):
    tq = _pick(tq_total, (256, 128))
    tk = _pick(tk_total, (768, 512, 256, 128))
    nq = tq_total // tq
    qblk = pl.BlockSpec((tq, HEAD_W), lambda b, h, i: (b * nq + i, h))
    kblk = pl.BlockSpec((tk_total, HEAD_W), lambda b, h, i: (b, h))
    return pl.pallas_call(
        functools.partial(_attn_kernel, tk=tk, n_kv=tk_total // tk, post=post),
        grid=(nb, DIFF_HEADS, nq),
        in_specs=[pl.BlockSpec(memory_space=pltpu.SMEM),
                  qblk, kblk, kblk, kblk,
                  pl.BlockSpec((1, HEAD_W), lambda b, h, i: (0, 0))],
        out_specs=qblk,
        out_shape=jax.ShapeDtypeStruct(q.shape, BF16),
        compiler_params=_cparams("parallel", "parallel", "parallel"),
        name="attention",
    )(lam, q, k1, k2, v, g)


def _outproj_kernel(ret_ref, dif_ref, w_ref, x_ref, mod_ref, g_ref, xo_ref, h_ref, *, row0, tpb, d):
    i = pl.program_id(0)
    r = row0 + i // tpb
    gate, shift, scale = _mod_rows(mod_ref, r, d, (2, 3, 4))
    half = w_ref.shape[0] // 2
    mix = (jnp.dot(ret_ref[...], w_ref[:half, :], preferred_element_type=F32)
           + jnp.dot(dif_ref[...], w_ref[half:, :], preferred_element_type=F32))
    x = x_ref[...] + gate * mix
    xo_ref[...] = x
    h_ref[...] = _norm_mod(x, g_ref[...], shift, scale).astype(BF16)


def _outproj(ret, dif, w_bf, x2, modl, g2, *, row0, tpb, tm):
    r, d = x2.shape
    row = lambda i: (i, 0)
    full = lambda i: (0, 0)
    return pl.pallas_call(
        functools.partial(_outproj_kernel, row0=row0, tpb=tpb, d=d),
        grid=(r // tm,),
        in_specs=[pl.BlockSpec((tm, ret.shape[1]), row),
                  pl.BlockSpec((tm, dif.shape[1]), row),
                  pl.BlockSpec(w_bf.shape, full),
                  pl.BlockSpec((tm, d), row),
                  pl.BlockSpec(modl.shape, full),
                  pl.BlockSpec((1, d), full)],
        out_specs=[pl.BlockSpec((tm, d), row), pl.BlockSpec((tm, d), row)],
        out_shape=[jax.ShapeDtypeStruct((r, d), F32), jax.ShapeDtypeStruct((r, d), BF16)],
        compiler_params=_cparams("parallel"),
        name="outproj",
    )(ret, dif, w_bf, x2, modl, g2)


def _swiglu_tile(h, wg_ref, wu_ref, wd_ref, fc):
    ff = wg_ref.shape[-1]
    acc = None
    for c0 in range(0, ff, fc):
        g = jnp.dot(h, wg_ref[0, :, c0:c0 + fc], preferred_element_type=F32)
        u = jnp.dot(h, wu_ref[0, :, c0:c0 + fc], preferred_element_type=F32)
        a = (_silu(g) * u).astype(BF16)
        part = jnp.dot(a, wd_ref[0, c0:c0 + fc, :], preferred_element_type=F32)
        acc = part if acc is None else acc + part
    return acc


def _ffn_dense_kernel(h_ref, wg_ref, wu_ref, wd_ref, x_ref, mod_ref, o_ref, *, row0, tpb, d, fc):
    i = pl.program_id(0)
    r = row0 + i // tpb
    (gate,) = _mod_rows(mod_ref, r, d, (5,))
    o_ref[...] = x_ref[...] + gate * _swiglu_tile(h_ref[...], wg_ref, wu_ref, wd_ref, fc)


def _ff_chunk(ff):
    return _pick(ff, (1408, 1024, 512, 256, 128))


def _ffn_dense(h, wg, wu, wd, x2, modl, *, row0, tpb, tm):
    r, d = x2.shape
    ff = wg.shape[-1]
    row = lambda i: (i, 0)
    wfull = lambda i: (0, 0, 0)
    return pl.pallas_call(
        functools.partial(_ffn_dense_kernel, row0=row0, tpb=tpb, d=d, fc=_ff_chunk(ff)),
        grid=(r // tm,),
        in_specs=[pl.BlockSpec((tm, d), row),
                  pl.BlockSpec((1, d, ff), wfull),
                  pl.BlockSpec((1, d, ff), wfull),
                  pl.BlockSpec((1, ff, d), wfull),
                  pl.BlockSpec((tm, d), row),
                  pl.BlockSpec(modl.shape, lambda i: (0, 0))],
        out_specs=pl.BlockSpec((tm, d), row),
        out_shape=jax.ShapeDtypeStruct((r, d), F32),
        compiler_params=_cparams("parallel"),
        name="ffn_dense",
    )(h, wg, wu, wd, x2, modl)


def _ffn_group_kernel(te_ref, nu_ref, h_ref, wg_ref, wu_ref, wd_ref, o_ref, *, fc):
    i = pl.program_id(0)

    @pl.when(i < nu_ref[0])
    def _():
        o_ref[...] = _swiglu_tile(h_ref[...], wg_ref, wu_ref, wd_ref, fc).astype(BF16)

    @pl.when(i >= nu_ref[0])
    def _():
        o_ref[...] = jnp.zeros(o_ref.shape, BF16)


def _ffn_grouped(tile_expert, n_used, hs, wg, wu, wd, *, tm):
    p, d = hs.shape
    ff = wg.shape[-1]
    row = lambda i, te, nu: (i, 0)
    wsel = lambda i, te, nu: (te[i], 0, 0)
    return pl.pallas_call(
        functools.partial(_ffn_group_kernel, fc=_ff_chunk(ff)),
        grid_spec=pltpu.PrefetchScalarGridSpec(
            num_scalar_prefetch=2,
            grid=(p // tm,),
            in_specs=[pl.BlockSpec((tm, d), row),
                      pl.BlockSpec((1, d, ff), wsel),
                      pl.BlockSpec((1, d, ff), wsel),
                      pl.BlockSpec((1, ff, d), wsel)],
            out_specs=pl.BlockSpec((tm, d), row)),
        out_shape=jax.ShapeDtypeStruct((p, d), BF16),
        compiler_params=_cparams("arbitrary"),
        name="ffn_grouped",
    )(tile_expert, n_used, hs, wg, wu, wd)


def _route_kernel(x_ref, mod_ref, g_ref, rw_ref, rb_ref, ri_ref, rwt_ref, cnt_ref, carry, *, tpb, d, ne):
    i = pl.program_id(0)

    @pl.when(i == 0)
    def _():
        carry[...] = jnp.zeros(carry.shape, F32)

    r = i // tpb
    shift, scale = _mod_rows(mod_ref, r, d, (3, 4))
    h = _norm_mod(x_ref[...], g_ref[...], shift, scale)
    tm = h.shape[0]
    nt = (((1,), (1,)), ((), ()))
    logits = lax.dot_general(rw_ref[...], h, nt, preferred_element_type=F32, precision=HIGHEST)
    logits = logits + rb_ref[...]
    eid = lax.broadcasted_iota(jnp.int32, (ne, tm), 0)
    m1 = jnp.max(logits, axis=0, keepdims=True)
    i1 = jnp.min(jnp.where(logits == m1, eid, ne), axis=0, keepdims=True)
    rest = jnp.where(eid == i1, -jnp.inf, logits)
    m2 = jnp.max(rest, axis=0, keepdims=True)
    i2 = jnp.min(jnp.where(rest == m2, eid, ne), axis=0, keepdims=True)
    e2 = jnp.exp(m2 - m1)
    w1 = 1.0 / (1.0 + e2)
    w2 = e2 * w1
    sel1 = eid == i1
    sel2 = eid == i2
    member = jnp.where(sel1, 1.0, jnp.where(sel2, 1.0, 0.0))
    tr = lax.broadcasted_iota(jnp.int32, (tm, tm), 0)
    tc = lax.broadcasted_iota(jnp.int32, (tm, tm), 1)
    before = jnp.where(tr < tc, 1.0, 0.0).astype(BF16)
    prefix = jnp.dot(member.astype(BF16), before, preferred_element_type=F32) + carry[:, 0:1]
    rank1 = jnp.sum(jnp.where(sel1, prefix, 0.0), axis=0, keepdims=True)
    rank2 = jnp.sum(jnp.where(sel2, prefix, 0.0), axis=0, keepdims=True)
    zi = jnp.zeros((SUBLANES - 4, tm), jnp.int32)
    ri_ref[...] = jnp.concatenate([i1, i2, rank1.astype(jnp.int32), rank2.astype(jnp.int32), zi], axis=0)
    rwt_ref[...] = jnp.concatenate([w1, w2, jnp.zeros((SUBLANES - 2, tm), F32)], axis=0)
    carry[...] = carry[...] + jnp.sum(member, axis=1, keepdims=True)
    cnt_ref[...] = carry[...]


def _route(x2, modl, g2, rw_t, rb, *, tpb, tm):
    n, d = x2.shape
    ne = rw_t.shape[0]
    return pl.pallas_call(
        functools.partial(_route_kernel, tpb=tpb, d=d, ne=ne),
        grid=(n // tm,),
        in_specs=[pl.BlockSpec((tm, d), lambda i: (i, 0)),
                  pl.BlockSpec(modl.shape, lambda i: (0, 0)),
                  pl.BlockSpec((1, d), lambda i: (0, 0)),
                  pl.BlockSpec((ne, d), lambda i: (0, 0)),
                  pl.BlockSpec((ne, 1), lambda i: (0, 0))],
        out_specs=[pl.BlockSpec((SUBLANES, tm), lambda i: (0, i)),
                   pl.BlockSpec((SUBLANES, tm), lambda i: (0, i)),
                   pl.BlockSpec((ne, LANES), lambda i: (0, 0))],
        out_shape=[jax.ShapeDtypeStruct((SUBLANES, n), jnp.int32),
                   jax.ShapeDtypeStruct((SUBLANES, n), F32),
                   jax.ShapeDtypeStruct((ne, LANES), F32)],
        scratch_shapes=[pltpu.VMEM((ne, LANES), F32)],
        compiler_params=_cparams("arbitrary"),
        name="route",
    )(x2, modl, g2, rw_t, rb)


def _pos_kernel(off_ref, ri_ref, pos_ref, *, ne):
    ri = ri_ref[...]
    e = ri[0:2, :]
    pos = ri[2:4, :]
    for k in range(ne):
        pos = pos + jnp.where(e == k, off_ref[k], 0)
    pos_ref[...] = pos


def _positions(offsets, route_i, *, ne):
    n = route_i.shape[1]
    return pl.pallas_call(
        functools.partial(_pos_kernel, ne=ne),
        in_specs=[pl.BlockSpec(memory_space=pltpu.SMEM),
                  pl.BlockSpec(route_i.shape, lambda: (0, 0))],
        out_specs=pl.BlockSpec((TOP_K, n), lambda: (0, 0)),
        out_shape=jax.ShapeDtypeStruct((TOP_K, n), jnp.int32),
        name="positions",
    )(offsets, route_i)


def _rowmove_kernel(idx_ref, src_ref, *rest, chunk, scatter):
    dst_ref, sem = rest[-2], rest[-1]
    base = pl.program_id(0) * chunk

    def copy(r):
        if scatter:
            return pltpu.make_async_copy(src_ref.at[r // TOP_K], dst_ref.at[idx_ref[r]], sem)
        return pltpu.make_async_copy(src_ref.at[idx_ref[r]], dst_ref.at[r], sem)

    def start(r, carry):
        copy(base + r).start()
        return carry

    def wait(r, carry):
        copy(base + r).wait()
        return carry

    lax.fori_loop(0, chunk, start, 0)
    lax.fori_loop(0, chunk, wait, 0)


def _rowmove(idx, src, dst_init, dst_rows, *, scatter):
    n = idx.shape[0]
    chunk = _pick(n, (1024, 512, 256, 128))
    any_spec = pl.BlockSpec(memory_space=pl.ANY)
    args = [idx, src] + ([dst_init] if dst_init is not None else [])
    return pl.pallas_call(
        functools.partial(_rowmove_kernel, chunk=chunk, scatter=scatter),
        grid_spec=pltpu.PrefetchScalarGridSpec(
            num_scalar_prefetch=1,
            grid=(n // chunk,),
            in_specs=[any_spec] * (len(args) - 1),
            out_specs=any_spec,
            scratch_shapes=[pltpu.SemaphoreType.DMA(())]),
        out_shape=jax.ShapeDtypeStruct((dst_rows,) + src.shape[1:], src.dtype),
        input_output_aliases={2: 0} if dst_init is not None else {},
        compiler_params=pltpu.CompilerParams(dimension_semantics=("arbitrary",), has_side_effects=True),
        name="rowmove_scatter" if scatter else "rowmove_gather",
    )(*args)


def _combine_kernel(y_ref, w_ref, x_ref, mod_ref, o_ref, *, tpb, d):
    i = pl.program_id(0)
    (gate,) = _mod_rows(mod_ref, i // tpb, d, (5,))
    w = w_ref[...]
    y = (w[:, 0:1] * y_ref[:, :d].astype(F32)) + (w[:, 1:2] * y_ref[:, d:].astype(F32))
    o_ref[...] = x_ref[...] + gate * y


def _combine(y2, w_t, x2, modl, *, tpb, tm):
    n, d = x2.shape
    row = lambda i: (i, 0)
    return pl.pallas_call(
        functools.partial(_combine_kernel, tpb=tpb, d=d),
        grid=(n // tm,),
        in_specs=[pl.BlockSpec((tm, TOP_K * d), row),
                  pl.BlockSpec((tm, TOP_K), row),
                  pl.BlockSpec((tm, d), row),
                  pl.BlockSpec(modl.shape, lambda i: (0, 0))],
        out_specs=pl.BlockSpec((tm, d), row),
        out_shape=jax.ShapeDtypeStruct((n, d), F32),
        compiler_params=_cparams("parallel"),
        name="combine",
    )(y2, w_t, x2, modl)


def _rope_tables(t):
    n_freq = QK_HEAD // 4
    rows = t // GRID_W
    row_pos = jnp.repeat(jnp.arange(rows, dtype=F32), GRID_W)
    col_pos = jnp.tile(jnp.arange(GRID_W, dtype=F32), rows)
    inv_freq = ROPE_BASE ** (-jnp.arange(n_freq, dtype=F32) / n_freq)
    ang = jnp.concatenate([row_pos[:, None] * inv_freq, col_pos[:, None] * inv_freq], axis=-1)
    ang = jnp.concatenate([ang, ang], axis=-1)
    sign = jnp.where(jnp.arange(QK_HEAD) < QK_HEAD // 2, -1.0, 1.0).astype(F32)
    rep = LANES // QK_HEAD
    return jnp.tile(jnp.cos(ang), (1, rep)), jnp.tile(jnp.sin(ang) * sign, (1, rep))


def _moe_plan(counts, n_tiles, tm):
    tiles = (counts + tm - 1) // tm
    ends = jnp.cumsum(tiles)
    offsets = (ends - tiles) * tm
    n_used = ends[-1]
    ids = jnp.arange(n_tiles, dtype=jnp.int32)
    te = jnp.searchsorted(ends, jnp.minimum(ids, n_used - 1), side="right").astype(jnp.int32)
    return offsets.astype(jnp.int32), te, n_used.reshape(1).astype(jnp.int32)


def kernel(x, c, ctx, c_ctx, ada_w, ada_b, norm1_g, norm2_g, w_in, ret_decay_fwd, ret_decay_bwd, ret_gn_g, diff_qn_g, diff_kn_g, lam_q1, lam_k1, lam_q2, lam_k2, diff_subln_g, w_out, ffn_w_gate, ffn_w_up, ffn_w_down, router_w, router_b, moe_w_gate, moe_w_up, moe_w_down):
    b, t, d = x.shape
    tc = ctx.shape[1]
    depth = ada_w.shape[0]
    ne = router_w.shape[-1]
    w = 4 * HEAD_W
    assert b + 1 <= SUBLANES and t % GRID_W == 0 and t % CHUNK == 0 and tc % CHUNK == 0
    assert w_in.shape[-1] == 7 * w and d % LANES == 0

    c8 = jnp.zeros((SUBLANES, d), F32).at[:b].set(c).at[b].set(c_ctx)
    mods = _ada(c8, ada_w, ada_b)

    cos, sin = _rope_tables(t)
    cos_c = jnp.ones((b * tc, LANES), F32)
    sin_c = jnp.zeros((b * tc, LANES), F32)
    tm = _pick(t, (512, 256, 128))
    tmc = _pick(b * tc, (512, 256, 128))
    tpb = t // tm
    ntc = (b * tc) // tmc
    rep4 = lambda g: jnp.tile(g.astype(F32), w // g.shape[0]).reshape(1, w)

    xl = x.reshape(b * t, d)
    xc = ctx.reshape(b * tc, d)
    zero_state = jnp.zeros((b, RET_HEADS, HEAD_W, HEAD_W), F32)

    for l in range(depth):
        ctx_out = l < depth - 1
        lam_init = 0.8 - 0.6 * math.exp(-0.3 * l)
        modl = mods[l]
        g1 = norm1_g[l].reshape(1, d)
        g2 = norm2_g[l].reshape(1, d)
        w_bf = w_in[l].astype(BF16)
        wo_bf = w_out[l].astype(BF16)
        qg, kg = rep4(diff_qn_g[l]), rep4(diff_kn_g[l])
        gn = ret_gn_g[l].reshape(1, w)
        sg = diff_subln_g[l].reshape(1, HEAD_W)
        lg = jnp.stack([jax.nn.log_sigmoid(ret_decay_fwd[l].astype(F32)),
                        jax.nn.log_sigmoid(ret_decay_bwd[l].astype(F32))])
        lam = (jnp.exp(jnp.sum(lam_q1[l].astype(F32) * lam_k1[l].astype(F32)))
               - jnp.exp(jnp.sum(lam_q2[l].astype(F32) * lam_k2[l].astype(F32))) + lam_init).reshape(1)
        post = 1.0 - lam_init

        pc = _inproj(xc, modl, g1, w_bf, qg, kg, cos_c, sin_c, row0=b, tpb=ntc, tm=tmc)
        pl_ = _inproj(xl, modl, g1, w_bf, qg, kg, cos, sin, row0=0, tpb=tpb, tm=tm)
        rqc, rkc, rvc, rgc, dqc, dk1c, dk2c, dvc = pc
        rq, rk, rv, rg, dq, dk1, dk2, dv = pl_

        ret_c, sc_f, sc_b = _retention(lg, rqc, rkc, rvc, rgc, gn, zero_state, zero_state, nb=b, t=tc)
        ret, _, _ = _retention(lg, rq, rk, rv, rg, gn, sc_f, sc_b, nb=b, t=t)

        cat = lambda a_c, a_l: jnp.concatenate(
            [a_c.reshape(b, tc, w), a_l.reshape(b, t, w)], axis=1).reshape(b * (tc + t), w)
        dif = _attention(lam, dq, cat(dk1c, dk1), cat(dk2c, dk2), cat(dvc, dv), sg,
                         nb=b, tq_total=t, tk_total=tc + t, post=post)
        xl, h2 = _outproj(ret, dif, wo_bf, xl, modl, g2, row0=0, tpb=tpb, tm=tm)

        if ctx_out:
            dif_c = _attention(lam, dqc, dk1c, dk2c, dvc, sg, nb=b, tq_total=tc, tk_total=tc, post=post)
            xc, h2c = _outproj(ret_c, dif_c, wo_bf, xc, modl, g2, row0=b, tpb=ntc, tm=tmc)

        if l % 2 == 0:
            j = l // 2
            wg = ffn_w_gate[j:j + 1].astype(BF16)
            wu = ffn_w_up[j:j + 1].astype(BF16)
            wd = ffn_w_down[j:j + 1].astype(BF16)
            xl = _ffn_dense(h2, wg, wu, wd, xl, modl, row0=0, tpb=tpb, tm=tm)
            if ctx_out:
                xc = _ffn_dense(h2c, wg, wu, wd, xc, modl, row0=b, tpb=ntc, tm=tmc)
        else:
            j = l // 2
            n = b * t
            tme = _pick(n, (256, 128))
            n_tiles = (n * TOP_K) // tme + ne
            route_i, route_w, counts = _route(xl, modl, g2, router_w[j].T.astype(F32),
                                              router_b[j].reshape(ne, 1).astype(F32), tpb=tpb, tm=tm)
            offsets, tile_expert, n_used = _moe_plan(counts[:, 0].astype(jnp.int32), n_tiles, tme)
            pos = _positions(offsets, route_i, ne=ne)
            pos_flat = pos.T.reshape(n * TOP_K)
            rows3 = lambda a: a.reshape(a.shape[0], d // LANES, LANES)
            hs = _rowmove(pos_flat, rows3(h2), jnp.zeros((n_tiles * tme, d // LANES, LANES), BF16),
                          n_tiles * tme, scatter=True)
            ys = _ffn_grouped(tile_expert, n_used, hs.reshape(n_tiles * tme, d),
                              moe_w_gate[j].astype(BF16), moe_w_up[j].astype(BF16),
                              moe_w_down[j].astype(BF16), tm=tme)
            y2 = _rowmove(pos_flat, rows3(ys), None, n * TOP_K, scatter=False)
            xl = _combine(y2.reshape(n, TOP_K * d), route_w[:TOP_K].T, xl, modl, tpb=tpb, tm=tm)
            if ctx_out:
                raise NotImplementedError("routed channel mixer on context tokens")
    return xl.reshape(b, t, d)
```

```python
import functools
import math

import jax
import jax.numpy as jnp
from jax import lax
from jax.experimental import pallas as pl
from jax.experimental.pallas import tpu as pltpu

F32 = jnp.float32
BF16 = jnp.bfloat16
HIGHEST = lax.Precision.HIGHEST

EPS = 1e-6
LOG2E = math.log2(math.e)
ROPE_BASE = 10000.0
GRID_W = 64
RET_HEADS = 4
DIFF_HEADS = 4
HEAD_W = 128
QK_HEAD = 64
CHUNK = 128
TOP_K = 2
LANES = 128
SUBLANES = 8
VMEM_LIMIT = 56 * 1024 * 1024


def _cparams(*sem):
    return pltpu.CompilerParams(dimension_semantics=sem, vmem_limit_bytes=VMEM_LIMIT)


def _silu(x):
    return x * (1.0 / (1.0 + jnp.exp(-x)))


def _pick(n, cands):
    for c in cands:
        if n % c == 0:
            return c
    raise ValueError(f"no tile for {n}")


def _ada_kernel(c_ref, w_ref, b_ref, o_ref):
    s = _silu(c_ref[...])
    o_ref[0] = jnp.dot(s, w_ref[0], preferred_element_type=F32, precision=HIGHEST) + b_ref[0]


def _ada(c8, ada_w, ada_b):
    depth, d, n = ada_w.shape
    tn = _pick(n, (1536, 1024, 512, 256, 128))
    return pl.pallas_call(
        _ada_kernel,
        grid=(depth, n // tn),
        in_specs=[pl.BlockSpec((SUBLANES, d), lambda l, j: (0, 0)),
                  pl.BlockSpec((1, d, tn), lambda l, j: (l, 0, j)),
                  pl.BlockSpec((1, 1, tn), lambda l, j: (l, 0, j))],
        out_specs=pl.BlockSpec((1, SUBLANES, tn), lambda l, j: (l, 0, j)),
        out_shape=jax.ShapeDtypeStruct((depth, SUBLANES, n), F32),
        compiler_params=_cparams("parallel", "parallel"),
        name="ada",
    )(c8, ada_w, ada_b.reshape(depth, 1, n))


def _mod_rows(mod_ref, r, d, ks):
    return [mod_ref[pl.ds(r, 1), k * d:(k + 1) * d] for k in ks]


def _norm_mod(x, g, shift, scale):
    ms = jnp.sum(x * x, axis=-1, keepdims=True) * (1.0 / x.shape[-1])
    y = x * lax.rsqrt(ms + EPS) * g
    return y * (1.0 + scale) + shift


def _inproj_kernel(x_ref, mod_ref, g_ref, w_ref, qg_ref, kg_ref, cos_ref, sin_ref,
                   rq_ref, rk_ref, rv_ref, rg_ref, dq_ref, dk1_ref, dk2_ref, dv_ref,
                   *, row0, tpb, d):
    i = pl.program_id(0)
    r = row0 + i // tpb
    shift, scale = _mod_rows(mod_ref, r, d, (0, 1))
    h = _norm_mod(x_ref[...], g_ref[...], shift, scale).astype(BF16)
    w = 4 * HEAD_W

    def proj(j):
        return jnp.dot(h, w_ref[:, j * w:(j + 1) * w], preferred_element_type=F32)

    rq_ref[...] = proj(0).astype(BF16)
    rk_ref[...] = (proj(1) * (HEAD_W ** -0.5)).astype(BF16)
    rv_ref[...] = proj(2).astype(BF16)
    rg_ref[...] = proj(3).astype(BF16)

    tm = h.shape[0]
    lane = lax.broadcasted_iota(jnp.int32, (tm, w), 1)
    rowi = lax.broadcasted_iota(jnp.int32, (w, w), 0) // QK_HEAD
    coli = lax.broadcasted_iota(jnp.int32, (w, w), 1) // QK_HEAD
    seg = jnp.where(rowi == coli, 1.0, 0.0).astype(BF16)
    low = (lane % QK_HEAD) < (QK_HEAD // 2)
    cos = jnp.concatenate([cos_ref[...]] * (w // LANES), axis=1)
    sin = jnp.concatenate([sin_ref[...]] * (w // LANES), axis=1)

    def qk_norm_rope(a, gain):
        sq = a * a
        hi = sq.astype(BF16)
        lo = (sq - hi.astype(F32)).astype(BF16)
        ss = (jnp.dot(hi, seg, preferred_element_type=F32)
              + jnp.dot(lo, seg, preferred_element_type=F32))
        y = a * lax.rsqrt(ss * (1.0 / QK_HEAD) + EPS) * gain
        rot = jnp.where(low, pltpu.roll(y, w - QK_HEAD // 2, 1), pltpu.roll(y, QK_HEAD // 2, 1))
        return y * cos + rot * sin

    q = qk_norm_rope(proj(4), qg_ref[...]) * (QK_HEAD ** -0.5 * LOG2E)
    dq_ref[...] = q.astype(BF16)
    kt = qk_norm_rope(proj(5), kg_ref[...]).T
    first = (lax.broadcasted_iota(jnp.int32, (w, tm), 0) % HEAD_W) < QK_HEAD
    dk1_ref[...] = jnp.where(first, kt, 0.0).astype(BF16)
    dk2_ref[...] = jnp.where(first, 0.0, kt).astype(BF16)
    v = proj(6).astype(BF16)
    ones_col = jnp.where(lax.broadcasted_iota(jnp.int32, (tm, HEAD_W), 1) == 0, 1.0, 0.0).astype(BF16)
    pieces = []
    for hd in range(DIFF_HEADS):
        pieces += [v[:, hd * HEAD_W:(hd + 1) * HEAD_W], ones_col]
    dv_ref[...] = jnp.concatenate(pieces, axis=1)


def _inproj(x2, modl, g1, w_bf, qg, kg, cos, sin, *, row0, tpb, tm):
    r, d = x2.shape
    w = 4 * HEAD_W
    n_tiles = r // tm
    row = lambda i: (i, 0)
    full = lambda i: (0, 0)
    tab = lambda i: (i % tpb, 0)
    out = jax.ShapeDtypeStruct((r, w), BF16)
    return pl.pallas_call(
        functools.partial(_inproj_kernel, row0=row0, tpb=tpb, d=d),
        grid=(n_tiles,),
        in_specs=[pl.BlockSpec((tm, d), row),
                  pl.BlockSpec(modl.shape, full),
                  pl.BlockSpec((1, d), full),
                  pl.BlockSpec(w_bf.shape, full),
                  pl.BlockSpec((1, w), full),
                  pl.BlockSpec((1, w), full),
                  pl.BlockSpec((tm, LANES), tab),
                  pl.BlockSpec((tm, LANES), tab)],
        out_specs=([pl.BlockSpec((tm, w), row)] * 5 + [pl.BlockSpec((w, tm), lambda i: (0, i))] * 2
                   + [pl.BlockSpec((tm, 2 * w), row)]),
        out_shape=[out] * 5 + [jax.ShapeDtypeStruct((w, r), BF16)] * 2 + [jax.ShapeDtypeStruct((r, 2 * w), BF16)],
        compiler_params=_cparams("parallel"),
        name="inproj",
    )(x2, modl, g1, w_bf, qg, kg, cos, sin)


def _ret_kernel(lg_ref, q_ref, k_ref, v_ref, g_ref, gn_ref, s0f_ref, s0b_ref,
                o_ref, sf_ref, sb_ref, of_scr, ob_scr, *, n_chunks):
    hd = pl.program_id(1)
    c = CHUNK
    lgf = jnp.full((c, c), lg_ref[0, hd], F32)
    lgb = jnp.full((c, c), lg_ref[1, hd], F32)
    row = lax.broadcasted_iota(jnp.int32, (c, c), 0).astype(F32)
    col = lax.broadcasted_iota(jnp.int32, (c, c), 1).astype(F32)
    rel = row - col
    dmat_f = jnp.where(rel >= 0, jnp.exp(lgf * jnp.maximum(rel, 0.0)), 0.0)
    dmat_b = jnp.where(rel <= 0, jnp.exp(lgb * jnp.maximum(-rel, 0.0)), 0.0)
    zeta_f = jnp.exp(lgf * (c - 1.0 - row))
    xi_f = jnp.exp(lgf * (row + 1.0))
    zeta_b = jnp.exp(lgb * row)
    xi_b = jnp.exp(lgb * (c - row))
    dec_f = jnp.exp(lgf * c)
    dec_b = jnp.exp(lgb * c)

    sf_ref[0, 0] = s0f_ref[0, 0]
    sb_ref[0, 0] = s0b_ref[0, 0]

    nt = (((1,), (1,)), ((), ()))

    def one(ci, dmat, zeta, xi, dec, s_ref, o_scr):
        rows = pl.ds(pl.multiple_of(ci * c, c), c)
        q = q_ref[rows, :]
        k = k_ref[rows, :]
        v = v_ref[rows, :]
        s = s_ref[0, 0]
        sc = lax.dot_general(q, k, nt, preferred_element_type=F32) * dmat
        o = jnp.dot(sc.astype(BF16), v, preferred_element_type=F32)
        o = o + xi * jnp.dot(q, s.astype(BF16), preferred_element_type=F32)
        o_scr[rows, :] = o
        kzt = (k.astype(F32) * zeta).T.astype(BF16)
        s_ref[0, 0] = dec * s + jnp.dot(kzt, v, preferred_element_type=F32)

    def body(i, carry):
        one(i, dmat_f, zeta_f, xi_f, dec_f, sf_ref, of_scr)
        one(n_chunks - 1 - i, dmat_b, zeta_b, xi_b, dec_b, sb_ref, ob_scr)
        return carry

    lax.fori_loop(0, n_chunks, body, 0)

    gn = gn_ref[...]

    def fin(ci, carry):
        rows = pl.ds(pl.multiple_of(ci * c, c), c)
        y = of_scr[rows, :] + ob_scr[rows, :]
        mu = jnp.sum(y, axis=-1, keepdims=True) * (1.0 / HEAD_W)
        yc = y - mu
        var = jnp.sum(yc * yc, axis=-1, keepdims=True) * (1.0 / HEAD_W)
        yn = yc * lax.rsqrt(var + EPS) * gn
        o_ref[rows, :] = (_silu(g_ref[rows, :].astype(F32)) * yn).astype(BF16)
        return carry

    lax.fori_loop(0, n_chunks, fin, 0)


def _retention(lg, rq, rk, rv, rg, gn, s0f, s0b, *, nb, t):
    n_chunks = t // CHUNK
    blk = pl.BlockSpec((t, HEAD_W), lambda b, h: (b, h))
    sblk = pl.BlockSpec((1, 1, HEAD_W, HEAD_W), lambda b, h: (b, h, 0, 0))
    s_shape = jax.ShapeDtypeStruct((nb, RET_HEADS, HEAD_W, HEAD_W), F32)
    return pl.pallas_call(
        functools.partial(_ret_kernel, n_chunks=n_chunks),
        grid=(nb, RET_HEADS),
        in_specs=[pl.BlockSpec(memory_space=pltpu.SMEM),
                  blk, blk, blk, blk,
                  pl.BlockSpec((1, HEAD_W), lambda b, h: (0, h)),
                  sblk, sblk],
        out_specs=[blk, sblk, sblk],
        out_shape=[jax.ShapeDtypeStruct(rq.shape, BF16), s_shape, s_shape],
        scratch_shapes=[pltpu.VMEM((t, HEAD_W), F32), pltpu.VMEM((t, HEAD_W), F32)],
        compiler_params=_cparams("parallel", "parallel"),
        name="retention",
    )(lg, rq, rk, rv, rg, gn, s0f, s0b)


def _attn_kernel(lam_ref, q_ref, k1_ref, k2_ref, v_ref, g_ref, o_ref, *, tk, post):
    q = q_ref[...]

    def component(kt_ref):
        m = a = None
        for c0 in range(0, kt_ref.shape[1], tk):
            s = jnp.dot(q, kt_ref[:, c0:c0 + tk], preferred_element_type=F32)
            mc = jnp.max(s, axis=-1, keepdims=True)
            mn = mc if m is None else jnp.maximum(m, mc)
            p = jnp.exp2(s - mn).astype(BF16)
            pv = jnp.dot(p, v_ref[c0:c0 + tk, :], preferred_element_type=F32)
            a = pv if m is None else jnp.exp2(m - mn) * a + pv
            m = mn
        return a[:, :HEAD_W] * (1.0 / a[:, HEAD_W:HEAD_W + 1])

    o = component(k1_ref) - lam_ref[0] * component(k2_ref)
    ms = jnp.sum(o * o, axis=-1, keepdims=True) * (1.0 / HEAD_W)
    o_ref[...] = (o * lax.rsqrt(ms + EPS) * g_ref[...] * post).astype(BF16)


def _attention(lam, q, k1, k2, vx, g, *, nb, tq_total, tk_total, post):
    tq = _pick(tq_total, (256, 128))
    nq = tq_total // tq
    tk = _pick(tk_total, (768, 512, 256, 128))
    qblk = pl.BlockSpec((tq, HEAD_W), lambda b, h, i: (b * nq + i, h))
    kblk = pl.BlockSpec((HEAD_W, tk_total), lambda b, h, i: (h, b))
    vblk = pl.BlockSpec((tk_total, 2 * HEAD_W), lambda b, h, i: (b, h))
    return pl.pallas_call(
        functools.partial(_attn_kernel, tk=tk, post=post),
        grid=(nb, DIFF_HEADS, nq),
        in_specs=[pl.BlockSpec(memory_space=pltpu.SMEM),
                  qblk, kblk, kblk, vblk,
                  pl.BlockSpec((1, HEAD_W), lambda b, h, i: (0, 0))],
        out_specs=qblk,
        out_shape=jax.ShapeDtypeStruct(q.shape, BF16),
        compiler_params=_cparams("parallel", "parallel", "parallel"),
        name="attention",
    )(lam, q, k1, k2, vx, g)


def _outproj_kernel(ret_ref, dif_ref, w_ref, x_ref, mod_ref, g_ref, xo_ref, h_ref, *, row0, tpb, d):
    i = pl.program_id(0)
    r = row0 + i // tpb
    gate, shift, scale = _mod_rows(mod_ref, r, d, (2, 3, 4))
    half = w_ref.shape[0] // 2
    mix = (jnp.dot(ret_ref[...], w_ref[:half, :], preferred_element_type=F32)
           + jnp.dot(dif_ref[...], w_ref[half:, :], preferred_element_type=F32))
    x = x_ref[...] + gate * mix
    xo_ref[...] = x
    h_ref[...] = _norm_mod(x, g_ref[...], shift, scale).astype(BF16)


def _outproj(ret, dif, w_bf, x2, modl, g2, *, row0, tpb, tm):
    r, d = x2.shape
    row = lambda i: (i, 0)
    full = lambda i: (0, 0)
    return pl.pallas_call(
        functools.partial(_outproj_kernel, row0=row0, tpb=tpb, d=d),
        grid=(r // tm,),
        in_specs=[pl.BlockSpec((tm, ret.shape[1]), row),
                  pl.BlockSpec((tm, dif.shape[1]), row),
                  pl.BlockSpec(w_bf.shape, full),
                  pl.BlockSpec((tm, d), row),
                  pl.BlockSpec(modl.shape, full),
                  pl.BlockSpec((1, d), full)],
        out_specs=[pl.BlockSpec((tm, d), row), pl.BlockSpec((tm, d), row)],
        out_shape=[jax.ShapeDtypeStruct((r, d), F32), jax.ShapeDtypeStruct((r, d), BF16)],
        compiler_params=_cparams("parallel"),
        name="outproj",
    )(ret, dif, w_bf, x2, modl, g2)


def _swiglu_tile(h, wg_ref, wu_ref, wd_ref, fc):
    ff = wg_ref.shape[-1]
    acc = None
    for c0 in range(0, ff, fc):
        g = jnp.dot(h, wg_ref[0, :, c0:c0 + fc], preferred_element_type=F32)
        u = jnp.dot(h, wu_ref[0, :, c0:c0 + fc], preferred_element_type=F32)
        a = (_silu(g) * u).astype(BF16)
        part = jnp.dot(a, wd_ref[0, c0:c0 + fc, :], preferred_element_type=F32)
        acc = part if acc is None else acc + part
    return acc


def _ffn_dense_kernel(h_ref, wg_ref, wu_ref, wd_ref, x_ref, mod_ref, o_ref, *, row0, tpb, d, fc):
    i = pl.program_id(0)
    r = row0 + i // tpb
    (gate,) = _mod_rows(mod_ref, r, d, (5,))
    o_ref[...] = x_ref[...] + gate * _swiglu_tile(h_ref[...], wg_ref, wu_ref, wd_ref, fc)


def _ff_chunk(ff):
    return _pick(ff, (1408, 1024, 512, 256, 128))


def _ffn_dense(h, wg, wu, wd, x2, modl, *, row0, tpb, tm):
    r, d = x2.shape
    ff = wg.shape[-1]
    row = lambda i: (i, 0)
    wfull = lambda i: (0, 0, 0)
    return pl.pallas_call(
        functools.partial(_ffn_dense_kernel, row0=row0, tpb=tpb, d=d, fc=_ff_chunk(ff)),
        grid=(r // tm,),
        in_specs=[pl.BlockSpec((tm, d), row),
                  pl.BlockSpec((1, d, ff), wfull),
                  pl.BlockSpec((1, d, ff), wfull),
                  pl.BlockSpec((1, ff, d), wfull),
                  pl.BlockSpec((tm, d), row),
                  pl.BlockSpec(modl.shape, lambda i: (0, 0))],
        out_specs=pl.BlockSpec((tm, d), row),
        out_shape=jax.ShapeDtypeStruct((r, d), F32),
        compiler_params=_cparams("parallel"),
        name="ffn_dense",
    )(h, wg, wu, wd, x2, modl)


def _ffn_group_kernel(te_ref, nu_ref, h_ref, wg_ref, wu_ref, wd_ref, o_ref, *, fc):
    i = pl.program_id(0)

    @pl.when(i < nu_ref[0])
    def _():
        o_ref[...] = _swiglu_tile(h_ref[...].astype(BF16), wg_ref, wu_ref, wd_ref, fc)

    @pl.when(i >= nu_ref[0])
    def _():
        o_ref[...] = jnp.zeros(o_ref.shape, F32)


def _ffn_grouped(tile_expert, n_used, hs, wg, wu, wd, *, tm):
    p, d = hs.shape
    ff = wg.shape[-1]
    row = lambda i, te, nu: (i, 0)
    wsel = lambda i, te, nu: (te[i], 0, 0)
    return pl.pallas_call(
        functools.partial(_ffn_group_kernel, fc=_ff_chunk(ff)),
        grid_spec=pltpu.PrefetchScalarGridSpec(
            num_scalar_prefetch=2,
            grid=(p // tm,),
            in_specs=[pl.BlockSpec((tm, d), row),
                      pl.BlockSpec((1, d, ff), wsel),
                      pl.BlockSpec((1, d, ff), wsel),
                      pl.BlockSpec((1, ff, d), wsel)],
            out_specs=pl.BlockSpec((tm, d), row)),
        out_shape=jax.ShapeDtypeStruct((p, d), F32),
        compiler_params=_cparams("arbitrary"),
        name="ffn_grouped",
    )(tile_expert, n_used, hs, wg, wu, wd)


def _route_kernel(x_ref, mod_ref, g_ref, rw_ref, rb_ref, ri_ref, rwt_ref, cnt_ref, carry, *, tpb, d, ne):
    i = pl.program_id(0)

    @pl.when(i == 0)
    def _():
        carry[...] = jnp.zeros(carry.shape, F32)

    r = i // tpb
    shift, scale = _mod_rows(mod_ref, r, d, (3, 4))
    h = _norm_mod(x_ref[...], g_ref[...], shift, scale)
    tm = h.shape[0]
    nt = (((1,), (1,)), ((), ()))
    logits = lax.dot_general(rw_ref[...], h, nt, preferred_element_type=F32, precision=HIGHEST)
    logits = logits + rb_ref[...]
    eid = lax.broadcasted_iota(jnp.int32, (ne, tm), 0)
    m1 = jnp.max(logits, axis=0, keepdims=True)
    i1 = jnp.min(jnp.where(logits == m1, eid, ne), axis=0, keepdims=True)
    rest = jnp.where(eid == i1, -jnp.inf, logits)
    m2 = jnp.max(rest, axis=0, keepdims=True)
    i2 = jnp.min(jnp.where(rest == m2, eid, ne), axis=0, keepdims=True)
    e2 = jnp.exp(m2 - m1)
    w1 = 1.0 / (1.0 + e2)
    w2 = e2 * w1
    sel1 = eid == i1
    sel2 = eid == i2
    member = jnp.where(sel1, 1.0, jnp.where(sel2, 1.0, 0.0))
    tr = lax.broadcasted_iota(jnp.int32, (tm, tm), 0)
    tc = lax.broadcasted_iota(jnp.int32, (tm, tm), 1)
    before = jnp.where(tr < tc, 1.0, 0.0).astype(BF16)
    prefix = jnp.dot(member.astype(BF16), before, preferred_element_type=F32) + carry[:, 0:1]
    rank1 = jnp.sum(jnp.where(sel1, prefix, 0.0), axis=0, keepdims=True)
    rank2 = jnp.sum(jnp.where(sel2, prefix, 0.0), axis=0, keepdims=True)
    zi = jnp.zeros((SUBLANES - 4, tm), jnp.int32)
    ri_ref[...] = jnp.concatenate([i1, i2, rank1.astype(jnp.int32), rank2.astype(jnp.int32), zi], axis=0)
    rwt_ref[...] = jnp.concatenate([w1, w2, jnp.zeros((SUBLANES - 2, tm), F32)], axis=0)
    carry[...] = carry[...] + jnp.sum(member, axis=1, keepdims=True)
    cnt_ref[...] = carry[...]


def _route(x2, modl, g2, rw_t, rb, *, tpb, tm):
    n, d = x2.shape
    ne = rw_t.shape[0]
    return pl.pallas_call(
        functools.partial(_route_kernel, tpb=tpb, d=d, ne=ne),
        grid=(n // tm,),
        in_specs=[pl.BlockSpec((tm, d), lambda i: (i, 0)),
                  pl.BlockSpec(modl.shape, lambda i: (0, 0)),
                  pl.BlockSpec((1, d), lambda i: (0, 0)),
                  pl.BlockSpec((ne, d), lambda i: (0, 0)),
                  pl.BlockSpec((ne, 1), lambda i: (0, 0))],
        out_specs=[pl.BlockSpec((SUBLANES, tm), lambda i: (0, i)),
                   pl.BlockSpec((SUBLANES, tm), lambda i: (0, i)),
                   pl.BlockSpec((ne, LANES), lambda i: (0, 0))],
        out_shape=[jax.ShapeDtypeStruct((SUBLANES, n), jnp.int32),
                   jax.ShapeDtypeStruct((SUBLANES, n), F32),
                   jax.ShapeDtypeStruct((ne, LANES), F32)],
        scratch_shapes=[pltpu.VMEM((ne, LANES), F32)],
        compiler_params=_cparams("arbitrary"),
        name="route",
    )(x2, modl, g2, rw_t, rb)


def _pos_kernel(off_ref, ri_ref, pos_ref, *, ne):
    ri = ri_ref[...]
    e = ri[0:2, :]
    pos = ri[2:4, :]
    for k in range(ne):
        pos = pos + jnp.where(e == k, off_ref[k], 0)
    pos_ref[...] = pos


def _positions(offsets, route_i, *, ne):
    n = route_i.shape[1]
    return pl.pallas_call(
        functools.partial(_pos_kernel, ne=ne),
        in_specs=[pl.BlockSpec(memory_space=pltpu.SMEM),
                  pl.BlockSpec(route_i.shape, lambda: (0, 0))],
        out_specs=pl.BlockSpec((TOP_K, n), lambda: (0, 0)),
        out_shape=jax.ShapeDtypeStruct((TOP_K, n), jnp.int32),
        name="positions",
    )(offsets, route_i)


def _dispatch_kernel(pos_ref, x_ref, mod_ref, g_ref, xs_in_ref, xs_ref, h_scr, sem, *, tpb, d, n):
    del xs_in_ref
    i = pl.program_id(0)
    tm = h_scr.shape[0]
    shift, scale = _mod_rows(mod_ref, i // tpb, d, (3, 4))
    h_scr[...] = _norm_mod(x_ref[...], g_ref[...], shift, scale)

    def copy(j, t):
        dst = pos_ref[j * n + i * tm + t]
        return pltpu.make_async_copy(h_scr.at[pl.ds(t, 1), :], xs_ref.at[pl.ds(dst, 1), :], sem)

    for j in range(TOP_K):
        lax.fori_loop(0, tm, lambda t, c, j=j: (copy(j, t).start(), c)[1], 0)
    for j in range(TOP_K):
        lax.fori_loop(0, tm, lambda t, c, j=j: (copy(j, t).wait(), c)[1], 0)


def _dispatch(pos_flat, x2, modl, g2, xs_init, *, tpb, tm):
    n, d = x2.shape
    return pl.pallas_call(
        functools.partial(_dispatch_kernel, tpb=tpb, d=d, n=n),
        grid_spec=pltpu.PrefetchScalarGridSpec(
            num_scalar_prefetch=1,
            grid=(n // tm,),
            in_specs=[pl.BlockSpec((tm, d), lambda i, pos: (i, 0)),
                      pl.BlockSpec(modl.shape, lambda i, pos: (0, 0)),
                      pl.BlockSpec((1, d), lambda i, pos: (0, 0)),
                      pl.BlockSpec(memory_space=pl.ANY)],
            out_specs=pl.BlockSpec(memory_space=pl.ANY),
            scratch_shapes=[pltpu.VMEM((tm, d), F32), pltpu.SemaphoreType.DMA(())]),
        out_shape=jax.ShapeDtypeStruct(xs_init.shape, F32),
        input_output_aliases={4: 0},
        compiler_params=pltpu.CompilerParams(dimension_semantics=("arbitrary",), vmem_limit_bytes=VMEM_LIMIT,
                                             has_side_effects=True),
        name="dispatch",
    )(pos_flat, x2, modl, g2, xs_init)


def _combine_kernel(pos_ref, ys_ref, w_ref, x_ref, mod_ref, o_ref, ybuf, sem, *, tpb, d, n, n_steps):
    i = pl.program_id(0)
    tm = x_ref.shape[0]

    def copy(step, slot, j, t):
        src = pos_ref[j * n + step * tm + t]
        return pltpu.make_async_copy(ys_ref.at[pl.ds(src, 1), :], ybuf.at[slot, j, pl.ds(t, 1), :], sem.at[slot])

    def fetch(step, slot):
        for j in range(TOP_K):
            lax.fori_loop(0, tm, lambda t, c, j=j: (copy(step, slot, j, t).start(), c)[1], 0)

    @pl.when(i == 0)
    def _():
        fetch(0, 0)

    @pl.when(i + 1 < n_steps)
    def _():
        fetch(i + 1, (i + 1) % 2)

    slot = i % 2
    for j in range(TOP_K):
        lax.fori_loop(0, tm, lambda t, c, j=j: (copy(i, slot, j, t).wait(), c)[1], 0)

    (gate,) = _mod_rows(mod_ref, i // tpb, d, (5,))
    w = w_ref[...]
    y = w[:, 0:1] * ybuf[slot, 0] + w[:, 1:2] * ybuf[slot, 1]
    o_ref[...] = x_ref[...] + gate * y


def _combine(pos_flat, ys, w_t, x2, modl, *, tpb, tm):
    n, d = x2.shape
    n_steps = n // tm
    row = lambda i, pos: (i, 0)
    return pl.pallas_call(
        functools.partial(_combine_kernel, tpb=tpb, d=d, n=n, n_steps=n_steps),
        grid_spec=pltpu.PrefetchScalarGridSpec(
            num_scalar_prefetch=1,
            grid=(n_steps,),
            in_specs=[pl.BlockSpec(memory_space=pl.ANY),
                      pl.BlockSpec((tm, TOP_K), row),
                      pl.BlockSpec((tm, d), row),
                      pl.BlockSpec(modl.shape, lambda i, pos: (0, 0))],
            out_specs=pl.BlockSpec((tm, d), row),
            scratch_shapes=[pltpu.VMEM((2, TOP_K, tm, d), F32), pltpu.SemaphoreType.DMA((2,))]),
        out_shape=jax.ShapeDtypeStruct((n, d), F32),
        compiler_params=pltpu.CompilerParams(dimension_semantics=("arbitrary",), vmem_limit_bytes=VMEM_LIMIT),
        name="combine",
    )(pos_flat, ys, w_t, x2, modl)


def _rope_tables(t):
    n_freq = QK_HEAD // 4
    rows = t // GRID_W
    row_pos = jnp.repeat(jnp.arange(rows, dtype=F32), GRID_W)
    col_pos = jnp.tile(jnp.arange(GRID_W, dtype=F32), rows)
    inv_freq = ROPE_BASE ** (-jnp.arange(n_freq, dtype=F32) / n_freq)
    ang = jnp.concatenate([row_pos[:, None] * inv_freq, col_pos[:, None] * inv_freq], axis=-1)
    ang = jnp.concatenate([ang, ang], axis=-1)
    sign = jnp.where(jnp.arange(QK_HEAD) < QK_HEAD // 2, -1.0, 1.0).astype(F32)
    rep = LANES // QK_HEAD
    return jnp.tile(jnp.cos(ang), (1, rep)), jnp.tile(jnp.sin(ang) * sign, (1, rep))


def _moe_plan(counts, n_tiles, tm):
    tiles = (counts + tm - 1) // tm
    ends = jnp.cumsum(tiles)
    offsets = (ends - tiles) * tm
    n_used = ends[-1]
    ids = jnp.arange(n_tiles, dtype=jnp.int32)
    te = jnp.sum(jnp.minimum(ids, n_used - 1)[:, None] >= ends[None, :], axis=1).astype(jnp.int32)
    return offsets.astype(jnp.int32), te, n_used.reshape(1).astype(jnp.int32)


def kernel(x, c, ctx, c_ctx, ada_w, ada_b, norm1_g, norm2_g, w_in, ret_decay_fwd, ret_decay_bwd, ret_gn_g, diff_qn_g, diff_kn_g, lam_q1, lam_k1, lam_q2, lam_k2, diff_subln_g, w_out, ffn_w_gate, ffn_w_up, ffn_w_down, router_w, router_b, moe_w_gate, moe_w_up, moe_w_down):
    b, t, d = x.shape
    tc = ctx.shape[1]
    depth = ada_w.shape[0]
    ne = router_w.shape[-1]
    w = 4 * HEAD_W
    assert b + 1 <= SUBLANES and t % GRID_W == 0 and t % CHUNK == 0 and tc % CHUNK == 0
    assert w_in.shape[-1] == 7 * w and d % LANES == 0

    c8 = jnp.zeros((SUBLANES, d), F32).at[:b].set(c).at[b].set(c_ctx)
    mods = _ada(c8, ada_w, ada_b)

    cos, sin = _rope_tables(t)
    cos_c = jnp.ones((b * tc, LANES), F32)
    sin_c = jnp.zeros((b * tc, LANES), F32)
    tm = _pick(t, (512, 256, 128))
    tmc = _pick(b * tc, (512, 256, 128))
    tpb = t // tm
    ntc = (b * tc) // tmc
    rep4 = lambda g: jnp.tile(g.astype(F32), w // g.shape[0]).reshape(1, w)

    xl = x.reshape(b * t, d)
    xc = ctx.reshape(b * tc, d)
    zero_state = jnp.zeros((b, RET_HEADS, HEAD_W, HEAD_W), F32)

    for l in range(depth):
        ctx_out = l < depth - 1
        lam_init = 0.8 - 0.6 * math.exp(-0.3 * l)
        modl = mods[l]
        g1 = norm1_g[l].reshape(1, d)
        g2 = norm2_g[l].reshape(1, d)
        w_bf = w_in[l].astype(BF16)
        wo_bf = w_out[l].astype(BF16)
        qg, kg = rep4(diff_qn_g[l]), rep4(diff_kn_g[l])
        gn = ret_gn_g[l].reshape(1, w)
        sg = diff_subln_g[l].reshape(1, HEAD_W)
        lg = jnp.stack([jax.nn.log_sigmoid(ret_decay_fwd[l].astype(F32)),
                        jax.nn.log_sigmoid(ret_decay_bwd[l].astype(F32))])
        lam = (jnp.exp(jnp.sum(lam_q1[l].astype(F32) * lam_k1[l].astype(F32)))
               - jnp.exp(jnp.sum(lam_q2[l].astype(F32) * lam_k2[l].astype(F32))) + lam_init).reshape(1)
        post = 1.0 - lam_init

        pc = _inproj(xc, modl, g1, w_bf, qg, kg, cos_c, sin_c, row0=b, tpb=ntc, tm=tmc)
        pl_ = _inproj(xl, modl, g1, w_bf, qg, kg, cos, sin, row0=0, tpb=tpb, tm=tm)
        rqc, rkc, rvc, rgc, dqc, dk1c, dk2c, dvc = pc
        rq, rk, rv, rg, dq, dk1, dk2, dv = pl_

        ret_c, sc_f, sc_b = _retention(lg, rqc, rkc, rvc, rgc, gn, zero_state, zero_state, nb=b, t=tc)
        ret, _, _ = _retention(lg, rq, rk, rv, rg, gn, sc_f, sc_b, nb=b, t=t)

        cat = lambda a_c, a_l: jnp.concatenate(
            [a_c.reshape(b, tc, -1), a_l.reshape(b, t, -1)], axis=1).reshape(b * (tc + t), -1)
        cat_t = lambda a_c, a_l: jnp.concatenate(
            [a_c.reshape(w, b, tc), a_l.reshape(w, b, t)], axis=2).reshape(w, b * (tc + t))
        dif = _attention(lam, dq, cat_t(dk1c, dk1), cat_t(dk2c, dk2), cat(dvc, dv), sg,
                         nb=b, tq_total=t, tk_total=tc + t, post=post)
        xl, h2 = _outproj(ret, dif, wo_bf, xl, modl, g2, row0=0, tpb=tpb, tm=tm)

        if ctx_out:
            dif_c = _attention(lam, dqc, dk1c, dk2c, dvc, sg, nb=b, tq_total=tc, tk_total=tc, post=post)
            xc, h2c = _outproj(ret_c, dif_c, wo_bf, xc, modl, g2, row0=b, tpb=ntc, tm=tmc)

        if l % 2 == 0:
            j = l // 2
            wg = ffn_w_gate[j:j + 1].astype(BF16)
            wu = ffn_w_up[j:j + 1].astype(BF16)
            wd = ffn_w_down[j:j + 1].astype(BF16)
            xl = _ffn_dense(h2, wg, wu, wd, xl, modl, row0=0, tpb=tpb, tm=tm)
            if ctx_out:
                xc = _ffn_dense(h2c, wg, wu, wd, xc, modl, row0=b, tpb=ntc, tm=tmc)
        else:
            j = l // 2
            n = b * t
            tme = _pick(n, (256, 128))
            n_tiles = (n * TOP_K) // tme + ne
            route_i, route_w, counts = _route(xl, modl, g2, router_w[j].T.astype(F32),
                                              router_b[j].reshape(ne, 1).astype(F32), tpb=tpb, tm=tm)
            offsets, tile_expert, n_used = _moe_plan(counts[:, 0].astype(jnp.int32), n_tiles, tme)
            pos = _positions(offsets, route_i, ne=ne)
            pos_flat = pos.reshape(TOP_K * n)
            xs = _dispatch(pos_flat, xl, modl, g2, jnp.zeros((n_tiles * tme, d), F32),
                           tpb=t // tme, tm=tme)
            ys = _ffn_grouped(tile_expert, n_used, xs,
                              moe_w_gate[j].astype(BF16), moe_w_up[j].astype(BF16),
                              moe_w_down[j].astype(BF16), tm=tme)
            xl = _combine(pos_flat, ys, route_w[:TOP_K].T, xl, modl, tpb=t // tme, tm=tme)
            if ctx_out:
                raise NotImplementedError("routed channel mixer on context tokens")
    return xl.reshape(b, t, d)
```

```python
import functools
import math

import jax
import jax.numpy as jnp
from jax import lax
from jax.experimental import pallas as pl
from jax.experimental.pallas import tpu as pltpu

F32 = jnp.float32
BF16 = jnp.bfloat16
HIGHEST = lax.Precision.HIGHEST

EPS = 1e-6
LOG2E = math.log2(math.e)
ROPE_BASE = 10000.0
GRID_W = 64
RET_HEADS = 4
DIFF_HEADS = 4
HEAD_W = 128
QK_HEAD = 64
CHUNK = 128
TOP_K = 2
LANES = 128
SUBLANES = 8
VMEM_LIMIT = 56 * 1024 * 1024


def _cparams(*sem):
    return pltpu.CompilerParams(dimension_semantics=sem, vmem_limit_bytes=VMEM_LIMIT)


def _silu(x):
    return x * (1.0 / (1.0 + jnp.exp(-x)))


def _pick(n, cands):
    for c in cands:
        if n % c == 0:
            return c
    raise ValueError(f"no tile for {n}")


def _ada_kernel(c_ref, w_ref, b_ref, o_ref):
    s = _silu(c_ref[...])
    o_ref[0] = jnp.dot(s, w_ref[0], preferred_element_type=F32, precision=HIGHEST) + b_ref[0]


def _ada(c8, ada_w, ada_b):
    depth, d, n = ada_w.shape
    tn = _pick(n, (1536, 1024, 512, 256, 128))
    return pl.pallas_call(
        _ada_kernel,
        grid=(depth, n // tn),
        in_specs=[pl.BlockSpec((SUBLANES, d), lambda l, j: (0, 0)),
                  pl.BlockSpec((1, d, tn), lambda l, j: (l, 0, j)),
                  pl.BlockSpec((1, 1, tn), lambda l, j: (l, 0, j))],
        out_specs=pl.BlockSpec((1, SUBLANES, tn), lambda l, j: (l, 0, j)),
        out_shape=jax.ShapeDtypeStruct((depth, SUBLANES, n), F32),
        compiler_params=_cparams("parallel", "parallel"),
        name="ada",
    )(c8, ada_w, ada_b.reshape(depth, 1, n))


def _mod_rows(mod_ref, r, d, ks):
    return [mod_ref[pl.ds(r, 1), k * d:(k + 1) * d] for k in ks]


def _norm_mod(x, g, shift, scale):
    ms = jnp.sum(x * x, axis=-1, keepdims=True) * (1.0 / x.shape[-1])
    y = x * lax.rsqrt(ms + EPS) * g
    return y * (1.0 + scale) + shift


def _inproj_kernel(x_ref, mod_ref, g_ref, w_ref, qg_ref, kg_ref, cos_ref, sin_ref, *rest, row0, mtpb, d, n_alias):
    rq_ref, rk_ref, rv_ref, rg_ref, dq_ref = rest[n_alias:n_alias + 5]
    kv_refs = rest[n_alias + 5:]
    i = pl.program_id(0)
    r = row0 + i // mtpb
    shift, scale = _mod_rows(mod_ref, r, d, (0, 1))
    h = _norm_mod(x_ref[...], g_ref[...], shift, scale).astype(BF16)
    w = 4 * HEAD_W

    def proj(j):
        return jnp.dot(h, w_ref[:, j * w:(j + 1) * w], preferred_element_type=F32)

    rq_ref[...] = proj(0).astype(BF16)
    rk_ref[...] = (proj(1) * (HEAD_W ** -0.5)).astype(BF16)
    rv_ref[...] = proj(2).astype(BF16)
    rg_ref[...] = proj(3).astype(BF16)

    tm = h.shape[0]
    lane = lax.broadcasted_iota(jnp.int32, (tm, w), 1)
    rowi = lax.broadcasted_iota(jnp.int32, (w, w), 0) // QK_HEAD
    coli = lax.broadcasted_iota(jnp.int32, (w, w), 1) // QK_HEAD
    seg = jnp.where(rowi == coli, 1.0, 0.0).astype(BF16)
    low = (lane % QK_HEAD) < (QK_HEAD // 2)
    cos = jnp.concatenate([cos_ref[...]] * (w // LANES), axis=1)
    sin = jnp.concatenate([sin_ref[...]] * (w // LANES), axis=1)

    def qk_norm_rope(a, gain):
        sq = a * a
        hi = sq.astype(BF16)
        lo = (sq - hi.astype(F32)).astype(BF16)
        ss = (jnp.dot(hi, seg, preferred_element_type=F32)
              + jnp.dot(lo, seg, preferred_element_type=F32))
        y = a * lax.rsqrt(ss * (1.0 / QK_HEAD) + EPS) * gain
        rot = jnp.where(low, pltpu.roll(y, w - QK_HEAD // 2, 1), pltpu.roll(y, QK_HEAD // 2, 1))
        return y * cos + rot * sin

    q = qk_norm_rope(proj(4), qg_ref[...]) * (QK_HEAD ** -0.5 * LOG2E)
    dq_ref[...] = q.astype(BF16)
    kt = qk_norm_rope(proj(5), kg_ref[...]).T
    first = (lax.broadcasted_iota(jnp.int32, (w, tm), 0) % HEAD_W) < QK_HEAD
    k1t = jnp.where(first, kt, 0.0).astype(BF16)
    k2t = jnp.where(first, 0.0, kt).astype(BF16)
    v = proj(6).astype(BF16)
    ones_col = jnp.where(lax.broadcasted_iota(jnp.int32, (tm, HEAD_W), 1) == 0, 1.0, 0.0).astype(BF16)
    pieces = []
    for hd in range(DIFF_HEADS):
        pieces += [v[:, hd * HEAD_W:(hd + 1) * HEAD_W], ones_col]
    vx = jnp.concatenate(pieces, axis=1)
    for s in range(0, len(kv_refs), 3):
        kv_refs[s][...] = k1t
        kv_refs[s + 1][...] = k2t
        kv_refs[s + 2][...] = vx


def _inproj(x2, modl, g1, w_bf, qg, kg, cos, sin, *, row0, mtpb, tpb, tm, nb, t_kv, shared=None, shared_off=0):
    r, d = x2.shape
    w = 4 * HEAD_W
    n_tiles = r // tm
    row = lambda i: (i, 0)
    full = lambda i: (0, 0)
    tab = lambda i: (i % tpb, 0)
    out = jax.ShapeDtypeStruct((r, w), BF16)

    def kv_specs(off):
        kt = pl.BlockSpec((None, w, tm), lambda i: (i // tpb, 0, off + i % tpb))
        return [kt, kt, pl.BlockSpec((None, tm, 2 * w), lambda i: (i // tpb, off + i % tpb, 0))]

    out_specs = [pl.BlockSpec((tm, w), row)] * 5
    out_shape = [out] * 5
    if t_kv:
        kt_shape = jax.ShapeDtypeStruct((nb, w, t_kv), BF16)
        out_specs += kv_specs(0)
        out_shape += [kt_shape, kt_shape, jax.ShapeDtypeStruct((nb, t_kv, 2 * w), BF16)]
    args = [x2, modl, g1, w_bf, qg, kg, cos, sin]
    in_specs = [pl.BlockSpec((tm, d), row),
                pl.BlockSpec(modl.shape, full),
                pl.BlockSpec((1, d), full),
                pl.BlockSpec(w_bf.shape, full),
                pl.BlockSpec((1, w), full),
                pl.BlockSpec((1, w), full),
                pl.BlockSpec((tm, LANES), tab),
                pl.BlockSpec((tm, LANES), tab)]
    aliases = {}
    if shared is not None:
        assert shared_off % tm == 0
        for s in shared:
            aliases[len(args)] = len(out_shape)
            args.append(s)
            in_specs.append(pl.BlockSpec(memory_space=pl.ANY))
            out_shape.append(jax.ShapeDtypeStruct(s.shape, s.dtype))
        out_specs += kv_specs(shared_off // tm)
    return pl.pallas_call(
        functools.partial(_inproj_kernel, row0=row0, mtpb=mtpb, d=d, n_alias=len(aliases)),
        grid=(n_tiles,),
        in_specs=in_specs,
        out_specs=out_specs,
        out_shape=out_shape,
        input_output_aliases=aliases,
        compiler_params=_cparams("parallel"),
        name="inproj",
    )(*args)


def _ret_kernel(lg_ref, q_ref, k_ref, v_ref, g_ref, gn_ref, s0f_ref, s0b_ref,
                o_ref, sf_ref, sb_ref, of_scr, ob_scr, *, c, n_chunks):
    hd = pl.program_id(1)
    lgf = lg_ref[0, hd]
    lgb = lg_ref[1, hd]
    rel = (lax.broadcasted_iota(jnp.int32, (c, c), 0) - lax.broadcasted_iota(jnp.int32, (c, c), 1)).astype(F32)
    dmat_f = jnp.where(rel >= 0, jnp.exp(lgf * jnp.maximum(rel, 0.0)), 0.0)
    dmat_b = jnp.where(rel <= 0, jnp.exp(lgb * jnp.maximum(-rel, 0.0)), 0.0)
    row = lax.broadcasted_iota(jnp.int32, (c, HEAD_W), 0).astype(F32)
    zeta_f = jnp.exp(lgf * (c - 1.0 - row))
    xi_f = jnp.exp(lgf * (row + 1.0))
    zeta_b = jnp.exp(lgb * row)
    xi_b = jnp.exp(lgb * (c - row))
    dec_f = jnp.exp(jnp.full((HEAD_W, HEAD_W), lgf * c, F32))
    dec_b = jnp.exp(jnp.full((HEAD_W, HEAD_W), lgb * c, F32))

    sf_ref[0, 0] = s0f_ref[0, 0]
    sb_ref[0, 0] = s0b_ref[0, 0]

    nt = (((1,), (1,)), ((), ()))

    def one(ci, dmat, zeta, xi, dec, s_ref, o_scr):
        rows = pl.ds(pl.multiple_of(ci * c, c), c)
        q = q_ref[rows, :]
        k = k_ref[rows, :]
        v = v_ref[rows, :]
        s = s_ref[0, 0]
        sc = lax.dot_general(q, k, nt, preferred_element_type=F32) * dmat
        o = jnp.dot(sc.astype(BF16), v, preferred_element_type=F32)
        o = o + xi * jnp.dot(q, s.astype(BF16), preferred_element_type=F32)
        o_scr[rows, :] = o
        kzt = (k.astype(F32) * zeta).T.astype(BF16)
        s_ref[0, 0] = dec * s + jnp.dot(kzt, v, preferred_element_type=F32)

    unroll = _pick(n_chunks, (2, 1))

    def body(i, carry):
        for u in range(unroll):
            ci = i * unroll + u
            one(ci, dmat_f, zeta_f, xi_f, dec_f, sf_ref, of_scr)
            one(n_chunks - 1 - ci, dmat_b, zeta_b, xi_b, dec_b, sb_ref, ob_scr)
        return carry

    lax.fori_loop(0, n_chunks // unroll, body, 0)

    gn = gn_ref[...]
    fr = unroll * c

    def fin(ci, carry):
        rows = pl.ds(pl.multiple_of(ci * fr, fr), fr)
        y = of_scr[rows, :] + ob_scr[rows, :]
        mu = jnp.sum(y, axis=-1, keepdims=True) * (1.0 / HEAD_W)
        yc = y - mu
        var = jnp.sum(yc * yc, axis=-1, keepdims=True) * (1.0 / HEAD_W)
        yn = yc * lax.rsqrt(var + EPS) * gn
        o_ref[rows, :] = (_silu(g_ref[rows, :].astype(F32)) * yn).astype(BF16)
        return carry

    lax.fori_loop(0, n_chunks // unroll, fin, 0)


def _retention(lg, rq, rk, rv, rg, gn, s0f, s0b, *, nb, t):
    c = _pick(t, (256, CHUNK))
    n_chunks = t // c
    blk = pl.BlockSpec((t, HEAD_W), lambda b, h: (b, h))
    sblk = pl.BlockSpec((1, 1, HEAD_W, HEAD_W), lambda b, h: (b, h, 0, 0))
    s_shape = jax.ShapeDtypeStruct((nb, RET_HEADS, HEAD_W, HEAD_W), F32)
    return pl.pallas_call(
        functools.partial(_ret_kernel, c=c, n_chunks=n_chunks),
        grid=(nb, RET_HEADS),
        in_specs=[pl.BlockSpec(memory_space=pltpu.SMEM),
                  blk, blk, blk, blk,
                  pl.BlockSpec((1, HEAD_W), lambda b, h: (0, h)),
                  sblk, sblk],
        out_specs=[blk, sblk, sblk],
        out_shape=[jax.ShapeDtypeStruct(rq.shape, BF16), s_shape, s_shape],
        scratch_shapes=[pltpu.VMEM((t, HEAD_W), F32), pltpu.VMEM((t, HEAD_W), F32)],
        compiler_params=_cparams("parallel", "parallel"),
        name="retention",
    )(lg, rq, rk, rv, rg, gn, s0f, s0b)


def _attn_kernel(lam_ref, q_ref, k1_ref, k2_ref, v_ref, g_ref, o_ref, *, tk, post):
    q = q_ref[...]

    def component(kt_ref):
        m = a = None
        for c0 in range(0, kt_ref.shape[1], tk):
            s = jnp.dot(q, kt_ref[:, c0:c0 + tk], preferred_element_type=F32)
            mc = jnp.max(s, axis=-1, keepdims=True)
            mn = mc if m is None else jnp.maximum(m, mc)
            p = jnp.exp2(s - mn).astype(BF16)
            pv = jnp.dot(p, v_ref[c0:c0 + tk, :], preferred_element_type=F32)
            a = pv if m is None else jnp.exp2(m - mn) * a + pv
            m = mn
        return a[:, :HEAD_W] * (1.0 / a[:, HEAD_W:HEAD_W + 1])

    o = component(k1_ref) - lam_ref[0] * component(k2_ref)
    ms = jnp.sum(o * o, axis=-1, keepdims=True) * (1.0 / HEAD_W)
    o_ref[...] = (o * lax.rsqrt(ms + EPS) * g_ref[...] * post).astype(BF16)


def _attention(lam, q, k1, k2, vx, g, *, nb, tq_total, tk_total, post):
    tq = _pick(tq_total, (512, 256, 128))
    nq = tq_total // tq
    tk = _pick(tk_total, (256, 128))
    qblk = pl.BlockSpec((tq, HEAD_W), lambda b, h, i: (b * nq + i, h))
    kblk = pl.BlockSpec((None, HEAD_W, tk_total), lambda b, h, i: (b, h, 0))
    vblk = pl.BlockSpec((None, tk_total, 2 * HEAD_W), lambda b, h, i: (b, 0, h))
    return pl.pallas_call(
        functools.partial(_attn_kernel, tk=tk, post=post),
        grid=(nb, DIFF_HEADS, nq),
        in_specs=[pl.BlockSpec(memory_space=pltpu.SMEM),
                  qblk, kblk, kblk, vblk,
                  pl.BlockSpec((1, HEAD_W), lambda b, h, i: (0, 0))],
        out_specs=qblk,
        out_shape=jax.ShapeDtypeStruct(q.shape, BF16),
        compiler_params=_cparams("parallel", "parallel", "parallel"),
        name="attention",
    )(lam, q, k1, k2, vx, g)


def _outproj_kernel(ret_ref, dif_ref, w_ref, x_ref, mod_ref, g_ref, xo_ref, h_ref, *, row0, tpb, d):
    i = pl.program_id(0)
    r = row0 + i // tpb
    gate, shift, scale = _mod_rows(mod_ref, r, d, (2, 3, 4))
    half = w_ref.shape[0] // 2
    mix = (jnp.dot(ret_ref[...], w_ref[:half, :], preferred_element_type=F32)
           + jnp.dot(dif_ref[...], w_ref[half:, :], preferred_element_type=F32))
    x = x_ref[...] + gate * mix
    xo_ref[...] = x
    h_ref[...] = _norm_mod(x, g_ref[...], shift, scale).astype(BF16)


def _outproj(ret, dif, w_bf, x2, modl, g2, *, row0, tpb, tm):
    r, d = x2.shape
    row = lambda i: (i, 0)
    full = lambda i: (0, 0)
    return pl.pallas_call(
        functools.partial(_outproj_kernel, row0=row0, tpb=tpb, d=d),
        grid=(r // tm,),
        in_specs=[pl.BlockSpec((tm, ret.shape[1]), row),
                  pl.BlockSpec((tm, dif.shape[1]), row),
                  pl.BlockSpec(w_bf.shape, full),
                  pl.BlockSpec((tm, d), row),
                  pl.BlockSpec(modl.shape, full),
                  pl.BlockSpec((1, d), full)],
        out_specs=[pl.BlockSpec((tm, d), row), pl.BlockSpec((tm, d), row)],
        out_shape=[jax.ShapeDtypeStruct((r, d), F32), jax.ShapeDtypeStruct((r, d), BF16)],
        compiler_params=_cparams("parallel"),
        name="outproj",
    )(ret, dif, w_bf, x2, modl, g2)


def _swiglu_tile(h, wg_ref, wu_ref, wd_ref, fc):
    ff = wg_ref.shape[-1]
    acc = None
    for c0 in range(0, ff, fc):
        g = jnp.dot(h, wg_ref[0, :, c0:c0 + fc], preferred_element_type=F32)
        u = jnp.dot(h, wu_ref[0, :, c0:c0 + fc], preferred_element_type=F32)
        a = (_silu(g) * u).astype(BF16)
        part = jnp.dot(a, wd_ref[0, c0:c0 + fc, :], preferred_element_type=F32)
        acc = part if acc is None else acc + part
    return acc


def _ffn_dense_kernel(h_ref, wg_ref, wu_ref, wd_ref, x_ref, mod_ref, o_ref, *, row0, tpb, d, fc):
    i = pl.program_id(0)
    r = row0 + i // tpb
    (gate,) = _mod_rows(mod_ref, r, d, (5,))
    o_ref[...] = x_ref[...] + gate * _swiglu_tile(h_ref[...], wg_ref, wu_ref, wd_ref, fc)


def _ff_chunk(ff):
    return _pick(ff, (256, 128))


def _ffn_dense(h, wg, wu, wd, x2, modl, *, row0, tpb, tm):
    r, d = x2.shape
    ff = wg.shape[-1]
    row = lambda i: (i, 0)
    wfull = lambda i: (0, 0, 0)
    return pl.pallas_call(
        functools.partial(_ffn_dense_kernel, row0=row0, tpb=tpb, d=d, fc=_ff_chunk(ff)),
        grid=(r // tm,),
        in_specs=[pl.BlockSpec((tm, d), row),
                  pl.BlockSpec((1, d, ff), wfull),
                  pl.BlockSpec((1, d, ff), wfull),
                  pl.BlockSpec((1, ff, d), wfull),
                  pl.BlockSpec((tm, d), row),
                  pl.BlockSpec(modl.shape, lambda i: (0, 0))],
        out_specs=pl.BlockSpec((tm, d), row),
        out_shape=jax.ShapeDtypeStruct((r, d), F32),
        compiler_params=_cparams("parallel"),
        name="ffn_dense",
    )(h, wg, wu, wd, x2, modl)


def _rows_to_wide(ref, tm, lead=()):
    return jnp.concatenate([ref[lead + (pl.ds(k, tm, stride=SUBLANES), slice(None))] for k in range(SUBLANES)],
                           axis=1)


def _wide_to_rows(ref, val):
    tm = val.shape[0]
    for k in range(SUBLANES):
        ref[pl.ds(k, tm, stride=SUBLANES), :] = val[:, k * LANES:(k + 1) * LANES]


def _row_tile(ref, r):
    return ref.at[pl.ds(pl.multiple_of(r * SUBLANES, SUBLANES), SUBLANES), :]


def _ffn_group_kernel(te_ref, nu_ref, h_ref, wg_ref, wu_ref, wd_ref, o_ref, *, tm, fc):
    i = pl.program_id(0)

    @pl.when(i < nu_ref[0])
    def _():
        _wide_to_rows(o_ref, _swiglu_tile(_rows_to_wide(h_ref, tm).astype(BF16), wg_ref, wu_ref, wd_ref, fc))

    @pl.when(i >= nu_ref[0])
    def _():
        o_ref[...] = jnp.zeros(o_ref.shape, F32)


def _ffn_grouped(tile_expert, n_used, hs, wg, wu, wd, *, tm):
    d, ff = wg.shape[-2:]
    assert d == SUBLANES * LANES
    rows = hs.shape[0] // SUBLANES
    row = lambda i, te, nu: (i, 0)
    wsel = lambda i, te, nu: (te[i], 0, 0)
    return pl.pallas_call(
        functools.partial(_ffn_group_kernel, tm=tm, fc=_ff_chunk(ff)),
        grid_spec=pltpu.PrefetchScalarGridSpec(
            num_scalar_prefetch=2,
            grid=(rows // tm,),
            in_specs=[pl.BlockSpec((tm * SUBLANES, LANES), row),
                      pl.BlockSpec((1, d, ff), wsel),
                      pl.BlockSpec((1, d, ff), wsel),
                      pl.BlockSpec((1, ff, d), wsel)],
            out_specs=pl.BlockSpec((tm * SUBLANES, LANES), row)),
        out_shape=jax.ShapeDtypeStruct(hs.shape, F32),
        compiler_params=_cparams("arbitrary"),
        name="ffn_grouped",
    )(tile_expert, n_used, hs, wg, wu, wd)


def _route_kernel(x_ref, mod_ref, g_ref, rw_ref, rb_ref, ri_ref, rwt_ref, cnt_ref, carry, *, tpb, d, ne):
    i = pl.program_id(0)

    @pl.when(i == 0)
    def _():
        carry[...] = jnp.zeros(carry.shape, F32)

    r = i // tpb
    shift, scale = _mod_rows(mod_ref, r, d, (3, 4))
    h = _norm_mod(x_ref[...], g_ref[...], shift, scale)
    tm = h.shape[0]
    nt = (((1,), (1,)), ((), ()))
    logits = lax.dot_general(rw_ref[...], h, nt, preferred_element_type=F32, precision=HIGHEST)
    logits = logits + rb_ref[...]
    eid = lax.broadcasted_iota(jnp.int32, (ne, tm), 0)
    m1 = jnp.max(logits, axis=0, keepdims=True)
    i1 = jnp.min(jnp.where(logits == m1, eid, ne), axis=0, keepdims=True)
    rest = jnp.where(eid == i1, -jnp.inf, logits)
    m2 = jnp.max(rest, axis=0, keepdims=True)
    i2 = jnp.min(jnp.where(rest == m2, eid, ne), axis=0, keepdims=True)
    e2 = jnp.exp(m2 - m1)
    w1 = 1.0 / (1.0 + e2)
    w2 = e2 * w1
    sel1 = eid == i1
    sel2 = eid == i2
    member = jnp.where(sel1, 1.0, jnp.where(sel2, 1.0, 0.0))
    tr = lax.broadcasted_iota(jnp.int32, (tm, tm), 0)
    tc = lax.broadcasted_iota(jnp.int32, (tm, tm), 1)
    before = jnp.where(tr < tc, 1.0, 0.0).astype(BF16)
    prefix = jnp.dot(member.astype(BF16), before, preferred_element_type=F32) + carry[:, 0:1]
    rank1 = jnp.sum(jnp.where(sel1, prefix, 0.0), axis=0, keepdims=True)
    rank2 = jnp.sum(jnp.where(sel2, prefix, 0.0), axis=0, keepdims=True)
    zi = jnp.zeros((SUBLANES - 4, tm), jnp.int32)
    ri_ref[...] = jnp.concatenate([i1, i2, rank1.astype(jnp.int32), rank2.astype(jnp.int32), zi], axis=0)
    rwt_ref[...] = jnp.concatenate([w1, w2, jnp.zeros((SUBLANES - 2, tm), F32)], axis=0)
    carry[...] = carry[...] + jnp.sum(member, axis=1, keepdims=True)
    cnt_ref[...] = carry[...]


def _route(x2, modl, g2, rw_t, rb, *, tpb, tm):
    n, d = x2.shape
    ne = rw_t.shape[0]
    return pl.pallas_call(
        functools.partial(_route_kernel, tpb=tpb, d=d, ne=ne),
        grid=(n // tm,),
        in_specs=[pl.BlockSpec((tm, d), lambda i: (i, 0)),
                  pl.BlockSpec(modl.shape, lambda i: (0, 0)),
                  pl.BlockSpec((1, d), lambda i: (0, 0)),
                  pl.BlockSpec((ne, d), lambda i: (0, 0)),
                  pl.BlockSpec((ne, 1), lambda i: (0, 0))],
        out_specs=[pl.BlockSpec((SUBLANES, tm), lambda i: (0, i)),
                   pl.BlockSpec((SUBLANES, tm), lambda i: (0, i)),
                   pl.BlockSpec((ne, LANES), lambda i: (0, 0))],
        out_shape=[jax.ShapeDtypeStruct((SUBLANES, n), jnp.int32),
                   jax.ShapeDtypeStruct((SUBLANES, n), F32),
                   jax.ShapeDtypeStruct((ne, LANES), F32)],
        scratch_shapes=[pltpu.VMEM((ne, LANES), F32)],
        compiler_params=_cparams("arbitrary"),
        name="route",
    )(x2, modl, g2, rw_t, rb)


def _pos_kernel(off_ref, ri_ref, pos_ref, *, ne):
    ri = ri_ref[...]
    e = ri[0:2, :]
    pos = ri[2:4, :]
    for k in range(ne):
        pos = pos + jnp.where(e == k, off_ref[k], 0)
    pos_ref[...] = pos


def _positions(offsets, route_i, *, ne):
    n = route_i.shape[1]
    return pl.pallas_call(
        functools.partial(_pos_kernel, ne=ne),
        in_specs=[pl.BlockSpec(memory_space=pltpu.SMEM),
                  pl.BlockSpec(route_i.shape, lambda: (0, 0))],
        out_specs=pl.BlockSpec((TOP_K, n), lambda: (0, 0)),
        out_shape=jax.ShapeDtypeStruct((TOP_K, n), jnp.int32),
        name="positions",
    )(offsets, route_i)


def _dispatch_kernel(pos_ref, x_ref, mod_ref, g_ref, xs_in_ref, xs_ref, h_scr, sem, *, tpb, d, n):
    del xs_in_ref
    i = pl.program_id(0)
    tm = x_ref.shape[0]
    shift, scale = _mod_rows(mod_ref, i // tpb, d, (3, 4))
    _wide_to_rows(h_scr, _norm_mod(x_ref[...], g_ref[...], shift, scale))

    def copy(j, t):
        dst = pos_ref[j * n + i * tm + t]
        return pltpu.make_async_copy(_row_tile(h_scr, t), _row_tile(xs_ref, dst), sem)

    for j in range(TOP_K):
        lax.fori_loop(0, tm, lambda t, c, j=j: (copy(j, t).start(), c)[1], 0)
    for j in range(TOP_K):
        lax.fori_loop(0, tm, lambda t, c, j=j: (copy(j, t).wait(), c)[1], 0)


def _dispatch(pos_flat, x2, modl, g2, xs_init, *, tpb, tm):
    n, d = x2.shape
    return pl.pallas_call(
        functools.partial(_dispatch_kernel, tpb=tpb, d=d, n=n),
        grid_spec=pltpu.PrefetchScalarGridSpec(
            num_scalar_prefetch=1,
            grid=(n // tm,),
            in_specs=[pl.BlockSpec((tm, d), lambda i, pos: (i, 0)),
                      pl.BlockSpec(modl.shape, lambda i, pos: (0, 0)),
                      pl.BlockSpec((1, d), lambda i, pos: (0, 0)),
                      pl.BlockSpec(memory_space=pl.ANY)],
            out_specs=pl.BlockSpec(memory_space=pl.ANY),
            scratch_shapes=[pltpu.VMEM((tm * SUBLANES, LANES), F32), pltpu.SemaphoreType.DMA(())]),
        out_shape=jax.ShapeDtypeStruct(xs_init.shape, F32),
        input_output_aliases={4: 0},
        compiler_params=pltpu.CompilerParams(dimension_semantics=("arbitrary",), vmem_limit_bytes=VMEM_LIMIT,
                                             has_side_effects=True),
        name="dispatch",
    )(pos_flat, x2, modl, g2, xs_init)


def _combine_kernel(pos_ref, ys_ref, w_ref, x_ref, mod_ref, o_ref, ybuf, sem, *, tpb, d, n, n_steps):
    i = pl.program_id(0)
    tm = x_ref.shape[0]

    def copy(step, slot, j, t):
        src = pos_ref[j * n + step * tm + t]
        return pltpu.make_async_copy(_row_tile(ys_ref, src), _row_tile(ybuf.at[slot, j], t), sem.at[slot])

    def fetch(step, slot):
        for j in range(TOP_K):
            lax.fori_loop(0, tm, lambda t, c, j=j: (copy(step, slot, j, t).start(), c)[1], 0)

    @pl.when(i == 0)
    def _():
        fetch(0, 0)

    @pl.when(i + 1 < n_steps)
    def _():
        fetch(i + 1, (i + 1) % 2)

    slot = i % 2
    for j in range(TOP_K):
        lax.fori_loop(0, tm, lambda t, c, j=j: (copy(i, slot, j, t).wait(), c)[1], 0)

    (gate,) = _mod_rows(mod_ref, i // tpb, d, (5,))
    w = w_ref[...]
    y = w[:, 0:1] * _rows_to_wide(ybuf, tm, (slot, 0)) + w[:, 1:2] * _rows_to_wide(ybuf, tm, (slot, 1))
    o_ref[...] = x_ref[...] + gate * y


def _combine(pos_flat, ys, w_t, x2, modl, *, tpb, tm):
    n, d = x2.shape
    n_steps = n // tm
    row = lambda i, pos: (i, 0)
    return pl.pallas_call(
        functools.partial(_combine_kernel, tpb=tpb, d=d, n=n, n_steps=n_steps),
        grid_spec=pltpu.PrefetchScalarGridSpec(
            num_scalar_prefetch=1,
            grid=(n_steps,),
            in_specs=[pl.BlockSpec(memory_space=pl.ANY),
                      pl.BlockSpec((tm, TOP_K), row),
                      pl.BlockSpec((tm, d), row),
                      pl.BlockSpec(modl.shape, lambda i, pos: (0, 0))],
            out_specs=pl.BlockSpec((tm, d), row),
            scratch_shapes=[pltpu.VMEM((2, TOP_K, tm * SUBLANES, LANES), F32), pltpu.SemaphoreType.DMA((2,))]),
        out_shape=jax.ShapeDtypeStruct((n, d), F32),
        compiler_params=pltpu.CompilerParams(dimension_semantics=("arbitrary",), vmem_limit_bytes=VMEM_LIMIT),
        name="combine",
    )(pos_flat, ys, w_t, x2, modl)


def _rope_tables(t):
    n_freq = QK_HEAD // 4
    rows = t // GRID_W
    row_pos = jnp.repeat(jnp.arange(rows, dtype=F32), GRID_W)
    col_pos = jnp.tile(jnp.arange(GRID_W, dtype=F32), rows)
    inv_freq = ROPE_BASE ** (-jnp.arange(n_freq, dtype=F32) / n_freq)
    ang = jnp.concatenate([row_pos[:, None] * inv_freq, col_pos[:, None] * inv_freq], axis=-1)
    ang = jnp.concatenate([ang, ang], axis=-1)
    sign = jnp.where(jnp.arange(QK_HEAD) < QK_HEAD // 2, -1.0, 1.0).astype(F32)
    rep = LANES // QK_HEAD
    return jnp.tile(jnp.cos(ang), (1, rep)), jnp.tile(jnp.sin(ang) * sign, (1, rep))


def _moe_plan(counts, n_tiles, tm):
    tiles = (counts + tm - 1) // tm
    ends = jnp.cumsum(tiles)
    offsets = (ends - tiles) * tm
    n_used = ends[-1]
    ids = jnp.arange(n_tiles, dtype=jnp.int32)
    te = jnp.sum(jnp.minimum(ids, n_used - 1)[:, None] >= ends[None, :], axis=1).astype(jnp.int32)
    return offsets.astype(jnp.int32), te, n_used.reshape(1).astype(jnp.int32)


def kernel(x, c, ctx, c_ctx, ada_w, ada_b, norm1_g, norm2_g, w_in, ret_decay_fwd, ret_decay_bwd, ret_gn_g, diff_qn_g, diff_kn_g, lam_q1, lam_k1, lam_q2, lam_k2, diff_subln_g, w_out, ffn_w_gate, ffn_w_up, ffn_w_down, router_w, router_b, moe_w_gate, moe_w_up, moe_w_down):
    b, t, d = x.shape
    tc = ctx.shape[1]
    depth = ada_w.shape[0]
    ne = router_w.shape[-1]
    w = 4 * HEAD_W
    assert b + 1 <= SUBLANES and t % GRID_W == 0 and t % CHUNK == 0 and tc % CHUNK == 0
    assert w_in.shape[-1] == 7 * w and d % LANES == 0

    c8 = jnp.zeros((SUBLANES, d), F32).at[:b].set(c).at[b].set(c_ctx)
    mods = _ada(c8, ada_w, ada_b)

    cos, sin = _rope_tables(t)
    cos_c = jnp.ones((b * tc, LANES), F32)
    sin_c = jnp.zeros((b * tc, LANES), F32)
    tm = _pick(t, (512, 256, 128))
    tmc = _pick(tc, (256, 128))
    tpb = t // tm
    ntc = (b * tc) // tmc
    assert t % tmc == 0
    rep4 = lambda g: jnp.tile(g.astype(F32), w // g.shape[0]).reshape(1, w)

    xl = x.reshape(b * t, d)
    xc = ctx.reshape(b * tc, d)
    zero_state = jnp.zeros((b, RET_HEADS, HEAD_W, HEAD_W), F32)

    for l in range(depth):
        ctx_out = l < depth - 1
        lam_init = 0.8 - 0.6 * math.exp(-0.3 * l)
        modl = mods[l]
        g1 = norm1_g[l].reshape(1, d)
        g2 = norm2_g[l].reshape(1, d)
        w_bf = w_in[l].astype(BF16)
        wo_bf = w_out[l].astype(BF16)
        qg, kg = rep4(diff_qn_g[l]), rep4(diff_kn_g[l])
        gn = ret_gn_g[l].reshape(1, w)
        sg = diff_subln_g[l].reshape(1, HEAD_W)
        lg = jnp.stack([jax.nn.log_sigmoid(ret_decay_fwd[l].astype(F32)),
                        jax.nn.log_sigmoid(ret_decay_bwd[l].astype(F32))])
        lam = (jnp.exp(jnp.sum(lam_q1[l].astype(F32) * lam_k1[l].astype(F32)))
               - jnp.exp(jnp.sum(lam_q2[l].astype(F32) * lam_k2[l].astype(F32))) + lam_init).reshape(1)
        post = 1.0 - lam_init

        rq, rk, rv, rg, dq, dk1, dk2, dv = _inproj(
            xl, modl, g1, w_bf, qg, kg, cos, sin, row0=0, mtpb=tpb, tpb=tpb, tm=tm, nb=b, t_kv=0,
            shared=(jnp.zeros((b, w, t + tc), BF16), jnp.zeros((b, w, t + tc), BF16),
                    jnp.zeros((b, t + tc, 2 * w), BF16)))
        rqc, rkc, rvc, rgc, dqc, dk1c, dk2c, dvc, dk1, dk2, dv = _inproj(
            xc, modl, g1, w_bf, qg, kg, cos_c, sin_c, row0=b, mtpb=ntc, tpb=tc // tmc, tm=tmc, nb=b, t_kv=tc,
            shared=(dk1, dk2, dv), shared_off=t)

        ret_c, sc_f, sc_b = _retention(lg, rqc, rkc, rvc, rgc, gn, zero_state, zero_state, nb=b, t=tc)
        ret, _, _ = _retention(lg, rq, rk, rv, rg, gn, sc_f, sc_b, nb=b, t=t)

        dif = _attention(lam, dq, dk1, dk2, dv, sg, nb=b, tq_total=t, tk_total=tc + t, post=post)
        xl, h2 = _outproj(ret, dif, wo_bf, xl, modl, g2, row0=0, tpb=tpb, tm=tm)

        if ctx_out:
            dif_c = _attention(lam, dqc, dk1c, dk2c, dvc, sg, nb=b, tq_total=tc, tk_total=tc, post=post)
            xc, h2c = _outproj(ret_c, dif_c, wo_bf, xc, modl, g2, row0=b, tpb=ntc, tm=tmc)

        if l % 2 == 0:
            j = l // 2
            wg = ffn_w_gate[j:j + 1].astype(BF16)
            wu = ffn_w_up[j:j + 1].astype(BF16)
            wd = ffn_w_down[j:j + 1].astype(BF16)
            xl = _ffn_dense(h2, wg, wu, wd, xl, modl, row0=0, tpb=tpb, tm=tm)
            if ctx_out:
                xc = _ffn_dense(h2c, wg, wu, wd, xc, modl, row0=b, tpb=ntc, tm=tmc)
        else:
            j = l // 2
            n = b * t
            tme = _pick(t, (512, 256, 128))
            n_tiles = (n * TOP_K) // tme + ne
            route_i, route_w, counts = _route(xl, modl, g2, router_w[j].T.astype(F32),
                                              router_b[j].reshape(ne, 1).astype(F32), tpb=tpb, tm=tm)
            offsets, tile_expert, n_used = _moe_plan(counts[:, 0].astype(jnp.int32), n_tiles, tme)
            pos = _positions(offsets, route_i, ne=ne)
            pos_flat = pos.reshape(TOP_K * n)
            xs = _dispatch(pos_flat, xl, modl, g2, jnp.zeros((n_tiles * tme * SUBLANES, LANES), F32),
                           tpb=t // tme, tm=tme)
            ys = _ffn_grouped(tile_expert, n_used, xs,
                              moe_w_gate[j].astype(BF16), moe_w_up[j].astype(BF16),
                              moe_w_down[j].astype(BF16), tm=tme)
            xl = _combine(pos_flat, ys, route_w[:TOP_K].T, xl, modl, tpb=t // tme, tm=tme)
            if ctx_out:
                raise NotImplementedError("routed channel mixer on context tokens")
    return xl.reshape(b, t, d)
```

```python
import functools
import math

import jax
import jax.numpy as jnp
from jax import lax
from jax.experimental import pallas as pl
from jax.experimental.pallas import tpu as pltpu

F32 = jnp.float32
BF16 = jnp.bfloat16
HIGHEST = lax.Precision.HIGHEST

EPS = 1e-6
LOG2E = math.log2(math.e)
ROPE_BASE = 10000.0
GRID_W = 64
RET_HEADS = 4
DIFF_HEADS = 4
HEAD_W = 128
QK_HEAD = 64
CHUNK = 128
TOP_K = 2
LANES = 128
SUBLANES = 8
VMEM_LIMIT = 56 * 1024 * 1024


def _cparams(*sem):
    return pltpu.CompilerParams(dimension_semantics=sem, vmem_limit_bytes=VMEM_LIMIT)


def _silu(x):
    return x * (1.0 / (1.0 + jnp.exp(-x)))


def _pick(n, cands):
    for c in cands:
        if n % c == 0:
            return c
    raise ValueError(f"no tile for {n}")


def _ada_kernel(c_ref, w_ref, b_ref, o_ref):
    s = _silu(c_ref[...])
    o_ref[0] = jnp.dot(s, w_ref[0], preferred_element_type=F32, precision=HIGHEST) + b_ref[0]


def _ada(c8, ada_w, ada_b):
    depth, d, n = ada_w.shape
    tn = _pick(n, (1536, 1024, 512, 256, 128))
    return pl.pallas_call(
        _ada_kernel,
        grid=(depth, n // tn),
        in_specs=[pl.BlockSpec((SUBLANES, d), lambda l, j: (0, 0)),
                  pl.BlockSpec((1, d, tn), lambda l, j: (l, 0, j)),
                  pl.BlockSpec((1, 1, tn), lambda l, j: (l, 0, j))],
        out_specs=pl.BlockSpec((1, SUBLANES, tn), lambda l, j: (l, 0, j)),
        out_shape=jax.ShapeDtypeStruct((depth, SUBLANES, n), F32),
        compiler_params=_cparams("parallel", "parallel"),
        name="ada",
    )(c8, ada_w, ada_b.reshape(depth, 1, n))


def _mod_rows(mod_ref, r, d, ks):
    return [mod_ref[pl.ds(r, 1), k * d:(k + 1) * d] for k in ks]


def _norm_mod(x, g, shift, scale):
    ms = jnp.sum(x * x, axis=-1, keepdims=True) * (1.0 / x.shape[-1])
    y = x * lax.rsqrt(ms + EPS) * g
    return y * (1.0 + scale) + shift


def _inproj_kernel(x_ref, mod_ref, g_ref, w_ref, qg_ref, kg_ref, cos_ref, sin_ref, *rest, row0, mtpb, d, n_alias):
    rq_ref, rk_ref, rv_ref, rg_ref, dq_ref = rest[n_alias:n_alias + 5]
    kv_refs = rest[n_alias + 5:]
    i = pl.program_id(0)
    r = row0 + i // mtpb
    shift, scale = _mod_rows(mod_ref, r, d, (0, 1))
    h = _norm_mod(x_ref[...], g_ref[...], shift, scale).astype(BF16)
    w = 4 * HEAD_W

    def proj(j):
        return jnp.dot(h, w_ref[:, j * w:(j + 1) * w], preferred_element_type=F32)

    rq_ref[...] = proj(0).astype(BF16)
    rk_ref[...] = (proj(1) * (HEAD_W ** -0.5)).astype(BF16)
    rv_ref[...] = proj(2).astype(BF16)
    rg_ref[...] = proj(3).astype(BF16)

    tm = h.shape[0]
    lane = lax.broadcasted_iota(jnp.int32, (tm, w), 1)
    rowi = lax.broadcasted_iota(jnp.int32, (w, w), 0) // QK_HEAD
    coli = lax.broadcasted_iota(jnp.int32, (w, w), 1) // QK_HEAD
    seg = jnp.where(rowi == coli, 1.0, 0.0).astype(BF16)
    low = (lane % QK_HEAD) < (QK_HEAD // 2)
    cos = jnp.concatenate([cos_ref[...]] * (w // LANES), axis=1)
    sin = jnp.concatenate([sin_ref[...]] * (w // LANES), axis=1)

    def qk_norm_rope(a, gain):
        sq = a * a
        hi = sq.astype(BF16)
        lo = (sq - hi.astype(F32)).astype(BF16)
        ss = (jnp.dot(hi, seg, preferred_element_type=F32)
              + jnp.dot(lo, seg, preferred_element_type=F32))
        y = a * lax.rsqrt(ss * (1.0 / QK_HEAD) + EPS) * gain
        rot = jnp.where(low, pltpu.roll(y, w - QK_HEAD // 2, 1), pltpu.roll(y, QK_HEAD // 2, 1))
        return y * cos + rot * sin

    q = qk_norm_rope(proj(4), qg_ref[...]) * (QK_HEAD ** -0.5 * LOG2E)
    dq_ref[...] = q.astype(BF16)
    kt = qk_norm_rope(proj(5), kg_ref[...]).T
    first = (lax.broadcasted_iota(jnp.int32, (w, tm), 0) % HEAD_W) < QK_HEAD
    k1t = jnp.where(first, kt, 0.0).astype(BF16)
    k2t = jnp.where(first, 0.0, kt).astype(BF16)
    v = proj(6).astype(BF16)
    ones_col = jnp.where(lax.broadcasted_iota(jnp.int32, (tm, HEAD_W), 1) == 0, 1.0, 0.0).astype(BF16)
    pieces = []
    for hd in range(DIFF_HEADS):
        pieces += [v[:, hd * HEAD_W:(hd + 1) * HEAD_W], ones_col]
    vx = jnp.concatenate(pieces, axis=1)
    for s in range(0, len(kv_refs), 3):
        kv_refs[s][...] = k1t
        kv_refs[s + 1][...] = k2t
        kv_refs[s + 2][...] = vx


def _inproj(x2, modl, g1, w_bf, qg, kg, cos, sin, *, row0, mtpb, tpb, tm, nb, t_kv, shared=None, shared_off=0):
    r, d = x2.shape
    w = 4 * HEAD_W
    n_tiles = r // tm
    row = lambda i: (i, 0)
    full = lambda i: (0, 0)
    tab = lambda i: (i % tpb, 0)
    out = jax.ShapeDtypeStruct((r, w), BF16)

    def kv_specs(off):
        kt = pl.BlockSpec((None, w, tm), lambda i: (i // tpb, 0, off + i % tpb))
        return [kt, kt, pl.BlockSpec((None, tm, 2 * w), lambda i: (i // tpb, off + i % tpb, 0))]

    out_specs = [pl.BlockSpec((tm, w), row)] * 5
    out_shape = [out] * 5
    if t_kv:
        kt_shape = jax.ShapeDtypeStruct((nb, w, t_kv), BF16)
        out_specs += kv_specs(0)
        out_shape += [kt_shape, kt_shape, jax.ShapeDtypeStruct((nb, t_kv, 2 * w), BF16)]
    args = [x2, modl, g1, w_bf, qg, kg, cos, sin]
    in_specs = [pl.BlockSpec((tm, d), row),
                pl.BlockSpec(modl.shape, full),
                pl.BlockSpec((1, d), full),
                pl.BlockSpec(w_bf.shape, full),
                pl.BlockSpec((1, w), full),
                pl.BlockSpec((1, w), full),
                pl.BlockSpec((tm, LANES), tab),
                pl.BlockSpec((tm, LANES), tab)]
    aliases = {}
    if shared is not None:
        assert shared_off % tm == 0
        for s in shared:
            aliases[len(args)] = len(out_shape)
            args.append(s)
            in_specs.append(pl.BlockSpec(memory_space=pl.ANY))
            out_shape.append(jax.ShapeDtypeStruct(s.shape, s.dtype))
        out_specs += kv_specs(shared_off // tm)
    return pl.pallas_call(
        functools.partial(_inproj_kernel, row0=row0, mtpb=mtpb, d=d, n_alias=len(aliases)),
        grid=(n_tiles,),
        in_specs=in_specs,
        out_specs=out_specs,
        out_shape=out_shape,
        input_output_aliases=aliases,
        compiler_params=_cparams("parallel"),
        name="inproj",
    )(*args)


def _ret_kernel(lg_ref, q_ref, k_ref, v_ref, g_ref, gn_ref, s0f_ref, s0b_ref,
                o_ref, sf_ref, sb_ref, of_scr, ob_scr, *, c, n_chunks):
    hd = pl.program_id(1)
    lgf = lg_ref[0, hd]
    lgb = lg_ref[1, hd]
    rel = (lax.broadcasted_iota(jnp.int32, (c, c), 0) - lax.broadcasted_iota(jnp.int32, (c, c), 1)).astype(F32)
    dmat_f = jnp.where(rel >= 0, jnp.exp(lgf * jnp.maximum(rel, 0.0)), 0.0)
    dmat_b = jnp.where(rel <= 0, jnp.exp(lgb * jnp.maximum(-rel, 0.0)), 0.0)
    row = lax.broadcasted_iota(jnp.int32, (c, HEAD_W), 0).astype(F32)
    zeta_f = jnp.exp(lgf * (c - 1.0 - row))
    xi_f = jnp.exp(lgf * (row + 1.0))
    zeta_b = jnp.exp(lgb * row)
    xi_b = jnp.exp(lgb * (c - row))
    dec_f = jnp.exp(jnp.full((HEAD_W, HEAD_W), lgf * c, F32))
    dec_b = jnp.exp(jnp.full((HEAD_W, HEAD_W), lgb * c, F32))

    sf_ref[0, 0] = s0f_ref[0, 0]
    sb_ref[0, 0] = s0b_ref[0, 0]

    nt = (((1,), (1,)), ((), ()))

    def one(ci, dmat, zeta, xi, dec, s_ref, o_scr):
        rows = pl.ds(pl.multiple_of(ci * c, c), c)
        q = q_ref[rows, :]
        k = k_ref[rows, :]
        v = v_ref[rows, :]
        s = s_ref[0, 0]
        sc = lax.dot_general(q, k, nt, preferred_element_type=F32) * dmat
        o = jnp.dot(sc.astype(BF16), v, preferred_element_type=F32)
        o = o + xi * jnp.dot(q, s.astype(BF16), preferred_element_type=F32)
        o_scr[rows, :] = o
        kzt = (k.astype(F32) * zeta).T.astype(BF16)
        s_ref[0, 0] = dec * s + jnp.dot(kzt, v, preferred_element_type=F32)

    unroll = _pick(n_chunks, (2, 1))

    def body(i, carry):
        for u in range(unroll):
            ci = i * unroll + u
            one(ci, dmat_f, zeta_f, xi_f, dec_f, sf_ref, of_scr)
            one(n_chunks - 1 - ci, dmat_b, zeta_b, xi_b, dec_b, sb_ref, ob_scr)
        return carry

    lax.fori_loop(0, n_chunks // unroll, body, 0)

    gn = gn_ref[...]
    fr = unroll * c

    def fin(ci, carry):
        rows = pl.ds(pl.multiple_of(ci * fr, fr), fr)
        y = of_scr[rows, :] + ob_scr[rows, :]
        mu = jnp.sum(y, axis=-1, keepdims=True) * (1.0 / HEAD_W)
        yc = y - mu
        var = jnp.sum(yc * yc, axis=-1, keepdims=True) * (1.0 / HEAD_W)
        yn = yc * lax.rsqrt(var + EPS) * gn
        o_ref[rows, :] = (_silu(g_ref[rows, :].astype(F32)) * yn).astype(BF16)
        return carry

    lax.fori_loop(0, n_chunks // unroll, fin, 0)


def _retention(lg, rq, rk, rv, rg, gn, s0f, s0b, *, nb, t):
    c = _pick(t, (256, CHUNK))
    n_chunks = t // c
    blk = pl.BlockSpec((t, HEAD_W), lambda b, h: (b, h))
    sblk = pl.BlockSpec((1, 1, HEAD_W, HEAD_W), lambda b, h: (b, h, 0, 0))
    s_shape = jax.ShapeDtypeStruct((nb, RET_HEADS, HEAD_W, HEAD_W), F32)
    return pl.pallas_call(
        functools.partial(_ret_kernel, c=c, n_chunks=n_chunks),
        grid=(nb, RET_HEADS),
        in_specs=[pl.BlockSpec(memory_space=pltpu.SMEM),
                  blk, blk, blk, blk,
                  pl.BlockSpec((1, HEAD_W), lambda b, h: (0, h)),
                  sblk, sblk],
        out_specs=[blk, sblk, sblk],
        out_shape=[jax.ShapeDtypeStruct(rq.shape, BF16), s_shape, s_shape],
        scratch_shapes=[pltpu.VMEM((t, HEAD_W), F32), pltpu.VMEM((t, HEAD_W), F32)],
        compiler_params=_cparams("parallel", "parallel"),
        name="retention",
    )(lg, rq, rk, rv, rg, gn, s0f, s0b)


def _attn_kernel(lam_ref, q_ref, k1_ref, k2_ref, v_ref, g_ref, o_ref, *, tk, post):
    q = q_ref[...]

    def component(kt_ref):
        m = a = None
        for c0 in range(0, kt_ref.shape[1], tk):
            s = jnp.dot(q, kt_ref[:, c0:c0 + tk], preferred_element_type=F32)
            mc = jnp.max(s, axis=-1, keepdims=True)
            mn = mc if m is None else jnp.maximum(m, mc)
            p = jnp.exp2(s - mn).astype(BF16)
            pv = jnp.dot(p, v_ref[c0:c0 + tk, :], preferred_element_type=F32)
            a = pv if m is None else jnp.exp2(m - mn) * a + pv
            m = mn
        return a[:, :HEAD_W] * (1.0 / a[:, HEAD_W:HEAD_W + 1])

    o = component(k1_ref) - lam_ref[0] * component(k2_ref)
    ms = jnp.sum(o * o, axis=-1, keepdims=True) * (1.0 / HEAD_W)
    o_ref[...] = (o * lax.rsqrt(ms + EPS) * g_ref[...] * post).astype(BF16)


def _attention(lam, q, k1, k2, vx, g, *, nb, tq_total, tk_total, post):
    tq = _pick(tq_total, (512, 256, 128))
    nq = tq_total // tq
    tk = _pick(tk_total, (256, 128))
    qblk = pl.BlockSpec((tq, HEAD_W), lambda b, h, i: (b * nq + i, h))
    kblk = pl.BlockSpec((None, HEAD_W, tk_total), lambda b, h, i: (b, h, 0))
    vblk = pl.BlockSpec((None, tk_total, 2 * HEAD_W), lambda b, h, i: (b, 0, h))
    return pl.pallas_call(
        functools.partial(_attn_kernel, tk=tk, post=post),
        grid=(nb, DIFF_HEADS, nq),
        in_specs=[pl.BlockSpec(memory_space=pltpu.SMEM),
                  qblk, kblk, kblk, vblk,
                  pl.BlockSpec((1, HEAD_W), lambda b, h, i: (0, 0))],
        out_specs=qblk,
        out_shape=jax.ShapeDtypeStruct(q.shape, BF16),
        compiler_params=_cparams("parallel", "parallel", "parallel"),
        name="attention",
    )(lam, q, k1, k2, vx, g)


def _outproj_kernel(ret_ref, dif_ref, w_ref, x_ref, mod_ref, g_ref, xo_ref, h_ref, *, row0, tpb, d):
    i = pl.program_id(0)
    r = row0 + i // tpb
    gate, shift, scale = _mod_rows(mod_ref, r, d, (2, 3, 4))
    half = w_ref.shape[0] // 2
    mix = (jnp.dot(ret_ref[...], w_ref[:half, :], preferred_element_type=F32)
           + jnp.dot(dif_ref[...], w_ref[half:, :], preferred_element_type=F32))
    x = x_ref[...] + gate * mix
    xo_ref[...] = x
    h_ref[...] = _norm_mod(x, g_ref[...], shift, scale).astype(BF16)


def _outproj(ret, dif, w_bf, x2, modl, g2, *, row0, tpb, tm):
    r, d = x2.shape
    row = lambda i: (i, 0)
    full = lambda i: (0, 0)
    return pl.pallas_call(
        functools.partial(_outproj_kernel, row0=row0, tpb=tpb, d=d),
        grid=(r // tm,),
        in_specs=[pl.BlockSpec((tm, ret.shape[1]), row),
                  pl.BlockSpec((tm, dif.shape[1]), row),
                  pl.BlockSpec(w_bf.shape, full),
                  pl.BlockSpec((tm, d), row),
                  pl.BlockSpec(modl.shape, full),
                  pl.BlockSpec((1, d), full)],
        out_specs=[pl.BlockSpec((tm, d), row), pl.BlockSpec((tm, d), row)],
        out_shape=[jax.ShapeDtypeStruct((r, d), F32), jax.ShapeDtypeStruct((r, d), BF16)],
        compiler_params=_cparams("parallel"),
        name="outproj",
    )(ret, dif, w_bf, x2, modl, g2)


def _swiglu_tile(h, wg_ref, wu_ref, wd_ref, fc):
    ff = wg_ref.shape[-1]
    acc = None
    for c0 in range(0, ff, fc):
        g = jnp.dot(h, wg_ref[0, :, c0:c0 + fc], preferred_element_type=F32)
        u = jnp.dot(h, wu_ref[0, :, c0:c0 + fc], preferred_element_type=F32)
        a = (_silu(g) * u).astype(BF16)
        part = jnp.dot(a, wd_ref[0, c0:c0 + fc, :], preferred_element_type=F32)
        acc = part if acc is None else acc + part
    return acc


def _ffn_dense_kernel(h_ref, wg_ref, wu_ref, wd_ref, x_ref, mod_ref, o_ref, *, row0, tpb, d, fc):
    i = pl.program_id(0)
    r = row0 + i // tpb
    (gate,) = _mod_rows(mod_ref, r, d, (5,))
    o_ref[...] = x_ref[...] + gate * _swiglu_tile(h_ref[...], wg_ref, wu_ref, wd_ref, fc)


def _ff_chunk(ff):
    return _pick(ff, (256, 128))


def _ffn_dense(h, wg, wu, wd, x2, modl, *, row0, tpb, tm):
    r, d = x2.shape
    ff = wg.shape[-1]
    row = lambda i: (i, 0)
    wfull = lambda i: (0, 0, 0)
    return pl.pallas_call(
        functools.partial(_ffn_dense_kernel, row0=row0, tpb=tpb, d=d, fc=_ff_chunk(ff)),
        grid=(r // tm,),
        in_specs=[pl.BlockSpec((tm, d), row),
                  pl.BlockSpec((1, d, ff), wfull),
                  pl.BlockSpec((1, d, ff), wfull),
                  pl.BlockSpec((1, ff, d), wfull),
                  pl.BlockSpec((tm, d), row),
                  pl.BlockSpec(modl.shape, lambda i: (0, 0))],
        out_specs=pl.BlockSpec((tm, d), row),
        out_shape=jax.ShapeDtypeStruct((r, d), F32),
        compiler_params=_cparams("parallel"),
        name="ffn_dense",
    )(h, wg, wu, wd, x2, modl)


def _rows_to_wide(ref, tm, lead=()):
    return jnp.concatenate([ref[lead + (pl.ds(k, tm, stride=SUBLANES), slice(None))] for k in range(SUBLANES)],
                           axis=1)


def _wide_to_rows(ref, val):
    tm = val.shape[0]
    for k in range(SUBLANES):
        ref[pl.ds(k, tm, stride=SUBLANES), :] = val[:, k * LANES:(k + 1) * LANES]


def _row_tile(ref, r):
    return ref.at[pl.ds(pl.multiple_of(r * SUBLANES, SUBLANES), SUBLANES), :]


ROW_DMA_UNROLL = 16


def _start_rows(n, copy):
    def body(i, carry):
        for u in range(ROW_DMA_UNROLL):
            copy(i * ROW_DMA_UNROLL + u).start(priority=u % 2)
        return carry
    lax.fori_loop(0, n // ROW_DMA_UNROLL, body, 0)


def _wait_rows(n, copy):
    def body(i, carry):
        for u in range(ROW_DMA_UNROLL):
            copy(i * ROW_DMA_UNROLL + u).wait()
        return carry
    lax.fori_loop(0, n // ROW_DMA_UNROLL, body, 0)


def _ffn_group_kernel(te_ref, nu_ref, h_ref, wg_ref, wu_ref, wd_ref, o_ref, *, tm, fc):
    i = pl.program_id(0)

    @pl.when(i < nu_ref[0])
    def _():
        _wide_to_rows(o_ref, _swiglu_tile(_rows_to_wide(h_ref, tm).astype(BF16), wg_ref, wu_ref, wd_ref, fc))

    @pl.when(i >= nu_ref[0])
    def _():
        o_ref[...] = jnp.zeros(o_ref.shape, F32)


def _ffn_grouped(tile_expert, n_used, hs, wg, wu, wd, *, tm):
    d, ff = wg.shape[-2:]
    assert d == SUBLANES * LANES
    rows = hs.shape[0] // SUBLANES
    row = lambda i, te, nu: (i, 0)
    wsel = lambda i, te, nu: (te[i], 0, 0)
    return pl.pallas_call(
        functools.partial(_ffn_group_kernel, tm=tm, fc=_ff_chunk(ff)),
        grid_spec=pltpu.PrefetchScalarGridSpec(
            num_scalar_prefetch=2,
            grid=(rows // tm,),
            in_specs=[pl.BlockSpec((tm * SUBLANES, LANES), row),
                      pl.BlockSpec((1, d, ff), wsel),
                      pl.BlockSpec((1, d, ff), wsel),
                      pl.BlockSpec((1, ff, d), wsel)],
            out_specs=pl.BlockSpec((tm * SUBLANES, LANES), row)),
        out_shape=jax.ShapeDtypeStruct(hs.shape, F32),
        compiler_params=_cparams("arbitrary"),
        name="ffn_grouped",
    )(tile_expert, n_used, hs, wg, wu, wd)


def _route_kernel(x_ref, mod_ref, g_ref, rw_ref, rb_ref, ri_ref, rwt_ref, cnt_ref, carry, *, tpb, d, ne):
    i = pl.program_id(0)

    @pl.when(i == 0)
    def _():
        carry[...] = jnp.zeros(carry.shape, F32)

    r = i // tpb
    shift, scale = _mod_rows(mod_ref, r, d, (3, 4))
    h = _norm_mod(x_ref[...], g_ref[...], shift, scale)
    tm = h.shape[0]
    nt = (((1,), (1,)), ((), ()))
    logits = lax.dot_general(rw_ref[...], h, nt, preferred_element_type=F32, precision=HIGHEST)
    logits = logits + rb_ref[...]
    eid = lax.broadcasted_iota(jnp.int32, (ne, tm), 0)
    m1 = jnp.max(logits, axis=0, keepdims=True)
    i1 = jnp.min(jnp.where(logits == m1, eid, ne), axis=0, keepdims=True)
    rest = jnp.where(eid == i1, -jnp.inf, logits)
    m2 = jnp.max(rest, axis=0, keepdims=True)
    i2 = jnp.min(jnp.where(rest == m2, eid, ne), axis=0, keepdims=True)
    e2 = jnp.exp(m2 - m1)
    w1 = 1.0 / (1.0 + e2)
    w2 = e2 * w1
    sel1 = eid == i1
    sel2 = eid == i2
    member = jnp.where(sel1, 1.0, jnp.where(sel2, 1.0, 0.0))
    tr = lax.broadcasted_iota(jnp.int32, (tm, tm), 0)
    tc = lax.broadcasted_iota(jnp.int32, (tm, tm), 1)
    before = jnp.where(tr < tc, 1.0, 0.0).astype(BF16)
    prefix = jnp.dot(member.astype(BF16), before, preferred_element_type=F32) + carry[:, 0:1]
    rank1 = jnp.sum(jnp.where(sel1, prefix, 0.0), axis=0, keepdims=True)
    rank2 = jnp.sum(jnp.where(sel2, prefix, 0.0), axis=0, keepdims=True)
    zi = jnp.zeros((SUBLANES - 4, tm), jnp.int32)
    ri_ref[...] = jnp.concatenate([i1, i2, rank1.astype(jnp.int32), rank2.astype(jnp.int32), zi], axis=0)
    rwt_ref[...] = jnp.concatenate([w1, w2, jnp.zeros((SUBLANES - 2, tm), F32)], axis=0)
    carry[...] = carry[...] + jnp.sum(member, axis=1, keepdims=True)
    cnt_ref[...] = carry[...]


def _route(x2, modl, g2, rw_t, rb, *, tpb, tm):
    n, d = x2.shape
    ne = rw_t.shape[0]
    return pl.pallas_call(
        functools.partial(_route_kernel, tpb=tpb, d=d, ne=ne),
        grid=(n // tm,),
        in_specs=[pl.BlockSpec((tm, d), lambda i: (i, 0)),
                  pl.BlockSpec(modl.shape, lambda i: (0, 0)),
                  pl.BlockSpec((1, d), lambda i: (0, 0)),
                  pl.BlockSpec((ne, d), lambda i: (0, 0)),
                  pl.BlockSpec((ne, 1), lambda i: (0, 0))],
        out_specs=[pl.BlockSpec((SUBLANES, tm), lambda i: (0, i)),
                   pl.BlockSpec((SUBLANES, tm), lambda i: (0, i)),
                   pl.BlockSpec((ne, LANES), lambda i: (0, 0))],
        out_shape=[jax.ShapeDtypeStruct((SUBLANES, n), jnp.int32),
                   jax.ShapeDtypeStruct((SUBLANES, n), F32),
                   jax.ShapeDtypeStruct((ne, LANES), F32)],
        scratch_shapes=[pltpu.VMEM((ne, LANES), F32)],
        compiler_params=_cparams("arbitrary"),
        name="route",
    )(x2, modl, g2, rw_t, rb)


def _pos_kernel(off_ref, ri_ref, pos_ref, *, ne):
    ri = ri_ref[...]
    e = ri[0:2, :]
    pos = ri[2:4, :]
    for k in range(ne):
        pos = pos + jnp.where(e == k, off_ref[k], 0)
    pos_ref[...] = pos


def _positions(offsets, route_i, *, ne):
    n = route_i.shape[1]
    return pl.pallas_call(
        functools.partial(_pos_kernel, ne=ne),
        in_specs=[pl.BlockSpec(memory_space=pltpu.SMEM),
                  pl.BlockSpec(route_i.shape, lambda: (0, 0))],
        out_specs=pl.BlockSpec((TOP_K, n), lambda: (0, 0)),
        out_shape=jax.ShapeDtypeStruct((TOP_K, n), jnp.int32),
        name="positions",
    )(offsets, route_i)


def _dispatch_kernel(pos_ref, x_ref, mod_ref, g_ref, xs_in_ref, xs_ref, h_scr, sem, *, tpb, d, n):
    del xs_in_ref
    i = pl.program_id(0)
    tm = x_ref.shape[0]
    shift, scale = _mod_rows(mod_ref, i // tpb, d, (3, 4))
    _wide_to_rows(h_scr, _norm_mod(x_ref[...], g_ref[...], shift, scale))

    def copy(j, t):
        dst = pos_ref[j * n + i * tm + t]
        return pltpu.make_async_copy(_row_tile(h_scr, t), _row_tile(xs_ref, dst), sem)

    for j in range(TOP_K):
        _start_rows(tm, functools.partial(copy, j))
    for j in range(TOP_K):
        _wait_rows(tm, functools.partial(copy, j))


def _dispatch(pos_flat, x2, modl, g2, xs_init, *, tpb, tm):
    n, d = x2.shape
    return pl.pallas_call(
        functools.partial(_dispatch_kernel, tpb=tpb, d=d, n=n),
        grid_spec=pltpu.PrefetchScalarGridSpec(
            num_scalar_prefetch=1,
            grid=(n // tm,),
            in_specs=[pl.BlockSpec((tm, d), lambda i, pos: (i, 0)),
                      pl.BlockSpec(modl.shape, lambda i, pos: (0, 0)),
                      pl.BlockSpec((1, d), lambda i, pos: (0, 0)),
                      pl.BlockSpec(memory_space=pl.ANY)],
            out_specs=pl.BlockSpec(memory_space=pl.ANY),
            scratch_shapes=[pltpu.VMEM((tm * SUBLANES, LANES), F32), pltpu.SemaphoreType.DMA(())]),
        out_shape=jax.ShapeDtypeStruct(xs_init.shape, F32),
        input_output_aliases={4: 0},
        compiler_params=pltpu.CompilerParams(dimension_semantics=("arbitrary",), vmem_limit_bytes=VMEM_LIMIT,
                                             has_side_effects=True),
        name="dispatch",
    )(pos_flat, x2, modl, g2, xs_init)


def _combine_kernel(pos_ref, ys_ref, w_ref, x_ref, mod_ref, o_ref, ybuf, sem, *, tpb, d, n, n_steps):
    i = pl.program_id(0)
    tm = x_ref.shape[0]

    def copy(step, slot, j, t):
        src = pos_ref[j * n + step * tm + t]
        return pltpu.make_async_copy(_row_tile(ys_ref, src), _row_tile(ybuf.at[slot, j], t), sem.at[slot])

    def fetch(step, slot):
        for j in range(TOP_K):
            _start_rows(tm, functools.partial(copy, step, slot, j))

    @pl.when(i == 0)
    def _():
        fetch(0, 0)

    @pl.when(i + 1 < n_steps)
    def _():
        fetch(i + 1, (i + 1) % 2)

    slot = i % 2
    for j in range(TOP_K):
        _wait_rows(tm, functools.partial(copy, i, slot, j))

    (gate,) = _mod_rows(mod_ref, i // tpb, d, (5,))
    w = w_ref[...]
    y = w[:, 0:1] * _rows_to_wide(ybuf, tm, (slot, 0)) + w[:, 1:2] * _rows_to_wide(ybuf, tm, (slot, 1))
    o_ref[...] = x_ref[...] + gate * y


def _combine(pos_flat, ys, w_t, x2, modl, *, tpb, tm):
    n, d = x2.shape
    n_steps = n // tm
    row = lambda i, pos: (i, 0)
    return pl.pallas_call(
        functools.partial(_combine_kernel, tpb=tpb, d=d, n=n, n_steps=n_steps),
        grid_spec=pltpu.PrefetchScalarGridSpec(
            num_scalar_prefetch=1,
            grid=(n_steps,),
            in_specs=[pl.BlockSpec(memory_space=pl.ANY),
                      pl.BlockSpec((tm, TOP_K), row),
                      pl.BlockSpec((tm, d), row),
                      pl.BlockSpec(modl.shape, lambda i, pos: (0, 0))],
            out_specs=pl.BlockSpec((tm, d), row),
            scratch_shapes=[pltpu.VMEM((2, TOP_K, tm * SUBLANES, LANES), F32), pltpu.SemaphoreType.DMA((2,))]),
        out_shape=jax.ShapeDtypeStruct((n, d), F32),
        compiler_params=pltpu.CompilerParams(dimension_semantics=("arbitrary",), vmem_limit_bytes=VMEM_LIMIT),
        name="combine",
    )(pos_flat, ys, w_t, x2, modl)


def _rope_tables(t):
    n_freq = QK_HEAD // 4
    rows = t // GRID_W
    row_pos = jnp.repeat(jnp.arange(rows, dtype=F32), GRID_W)
    col_pos = jnp.tile(jnp.arange(GRID_W, dtype=F32), rows)
    inv_freq = ROPE_BASE ** (-jnp.arange(n_freq, dtype=F32) / n_freq)
    ang = jnp.concatenate([row_pos[:, None] * inv_freq, col_pos[:, None] * inv_freq], axis=-1)
    ang = jnp.concatenate([ang, ang], axis=-1)
    sign = jnp.where(jnp.arange(QK_HEAD) < QK_HEAD // 2, -1.0, 1.0).astype(F32)
    rep = LANES // QK_HEAD
    return jnp.tile(jnp.cos(ang), (1, rep)), jnp.tile(jnp.sin(ang) * sign, (1, rep))


def _moe_plan(counts, n_tiles, tm):
    tiles = (counts + tm - 1) // tm
    ends = jnp.cumsum(tiles)
    offsets = (ends - tiles) * tm
    n_used = ends[-1]
    ids = jnp.arange(n_tiles, dtype=jnp.int32)
    te = jnp.sum(jnp.minimum(ids, n_used - 1)[:, None] >= ends[None, :], axis=1).astype(jnp.int32)
    return offsets.astype(jnp.int32), te, n_used.reshape(1).astype(jnp.int32)


def kernel(x, c, ctx, c_ctx, ada_w, ada_b, norm1_g, norm2_g, w_in, ret_decay_fwd, ret_decay_bwd, ret_gn_g, diff_qn_g, diff_kn_g, lam_q1, lam_k1, lam_q2, lam_k2, diff_subln_g, w_out, ffn_w_gate, ffn_w_up, ffn_w_down, router_w, router_b, moe_w_gate, moe_w_up, moe_w_down):
    b, t, d = x.shape
    tc = ctx.shape[1]
    depth = ada_w.shape[0]
    ne = router_w.shape[-1]
    w = 4 * HEAD_W
    assert b + 1 <= SUBLANES and t % GRID_W == 0 and t % CHUNK == 0 and tc % CHUNK == 0
    assert w_in.shape[-1] == 7 * w and d % LANES == 0

    c8 = jnp.zeros((SUBLANES, d), F32).at[:b].set(c).at[b].set(c_ctx)
    mods = _ada(c8, ada_w, ada_b)

    cos, sin = _rope_tables(t)
    cos_c = jnp.ones((b * tc, LANES), F32)
    sin_c = jnp.zeros((b * tc, LANES), F32)
    tm = _pick(t, (512, 256, 128))
    tmc = _pick(tc, (256, 128))
    tpb = t // tm
    ntc = (b * tc) // tmc
    assert t % tmc == 0
    rep4 = lambda g: jnp.tile(g.astype(F32), w // g.shape[0]).reshape(1, w)

    xl = x.reshape(b * t, d)
    xc = ctx.reshape(b * tc, d)
    zero_state = jnp.zeros((b, RET_HEADS, HEAD_W, HEAD_W), F32)

    for l in range(depth):
        ctx_out = l < depth - 1
        lam_init = 0.8 - 0.6 * math.exp(-0.3 * l)
        modl = mods[l]
        g1 = norm1_g[l].reshape(1, d)
        g2 = norm2_g[l].reshape(1, d)
        w_bf = w_in[l].astype(BF16)
        wo_bf = w_out[l].astype(BF16)
        qg, kg = rep4(diff_qn_g[l]), rep4(diff_kn_g[l])
        gn = ret_gn_g[l].reshape(1, w)
        sg = diff_subln_g[l].reshape(1, HEAD_W)
        lg = jnp.stack([jax.nn.log_sigmoid(ret_decay_fwd[l].astype(F32)),
                        jax.nn.log_sigmoid(ret_decay_bwd[l].astype(F32))])
        lam = (jnp.exp(jnp.sum(lam_q1[l].astype(F32) * lam_k1[l].astype(F32)))
               - jnp.exp(jnp.sum(lam_q2[l].astype(F32) * lam_k2[l].astype(F32))) + lam_init).reshape(1)
        post = 1.0 - lam_init

        rq, rk, rv, rg, dq, dk1, dk2, dv = _inproj(
            xl, modl, g1, w_bf, qg, kg, cos, sin, row0=0, mtpb=tpb, tpb=tpb, tm=tm, nb=b, t_kv=0,
            shared=(jnp.zeros((b, w, t + tc), BF16), jnp.zeros((b, w, t + tc), BF16),
                    jnp.zeros((b, t + tc, 2 * w), BF16)))
        rqc, rkc, rvc, rgc, dqc, dk1c, dk2c, dvc, dk1, dk2, dv = _inproj(
            xc, modl, g1, w_bf, qg, kg, cos_c, sin_c, row0=b, mtpb=ntc, tpb=tc // tmc, tm=tmc, nb=b, t_kv=tc,
            shared=(dk1, dk2, dv), shared_off=t)

        ret_c, sc_f, sc_b = _retention(lg, rqc, rkc, rvc, rgc, gn, zero_state, zero_state, nb=b, t=tc)
        ret, _, _ = _retention(lg, rq, rk, rv, rg, gn, sc_f, sc_b, nb=b, t=t)

        dif = _attention(lam, dq, dk1, dk2, dv, sg, nb=b, tq_total=t, tk_total=tc + t, post=post)
        xl, h2 = _outproj(ret, dif, wo_bf, xl, modl, g2, row0=0, tpb=tpb, tm=tm)

        if ctx_out:
            dif_c = _attention(lam, dqc, dk1c, dk2c, dvc, sg, nb=b, tq_total=tc, tk_total=tc, post=post)
            xc, h2c = _outproj(ret_c, dif_c, wo_bf, xc, modl, g2, row0=b, tpb=ntc, tm=tmc)

        if l % 2 == 0:
            j = l // 2
            wg = ffn_w_gate[j:j + 1].astype(BF16)
            wu = ffn_w_up[j:j + 1].astype(BF16)
            wd = ffn_w_down[j:j + 1].astype(BF16)
            xl = _ffn_dense(h2, wg, wu, wd, xl, modl, row0=0, tpb=tpb, tm=tm)
            if ctx_out:
                xc = _ffn_dense(h2c, wg, wu, wd, xc, modl, row0=b, tpb=ntc, tm=tmc)
        else:
            j = l // 2
            n = b * t
            tme = _pick(t, (512, 256, 128))
            n_tiles = (n * TOP_K) // tme + ne
            route_i, route_w, counts = _route(xl, modl, g2, router_w[j].T.astype(F32),
                                              router_b[j].reshape(ne, 1).astype(F32), tpb=tpb, tm=tm)
            offsets, tile_expert, n_used = _moe_plan(counts[:, 0].astype(jnp.int32), n_tiles, tme)
            pos = _positions(offsets, route_i, ne=ne)
            pos_flat = pos.reshape(TOP_K * n)
            xs = _dispatch(pos_flat, xl, modl, g2, jnp.zeros((n_tiles * tme * SUBLANES, LANES), F32),
                           tpb=t // tme, tm=tme)
            ys = _ffn_grouped(tile_expert, n_used, xs,
                              moe_w_gate[j].astype(BF16), moe_w_up[j].astype(BF16),
                              moe_w_down[j].astype(BF16), tm=tme)
            xl = _combine(pos_flat, ys, route_w[:TOP_K].T, xl, modl, tpb=t // tme, tm=tme)
            if ctx_out:
                raise NotImplementedError("routed channel mixer on context tokens")
    return xl.reshape(b, t, d)
```

```python
import functools
import math

import jax
import jax.numpy as jnp
from jax import lax
from jax.experimental import pallas as pl
from jax.experimental.pallas import tpu as pltpu

F32 = jnp.float32
BF16 = jnp.bfloat16
HIGHEST = lax.Precision.HIGHEST

EPS = 1e-6
LOG2E = math.log2(math.e)
ROPE_BASE = 10000.0
GRID_W = 64
RET_HEADS = 4
DIFF_HEADS = 4
HEAD_W = 128
QK_HEAD = 64
CHUNK = 128
TOP_K = 2
LANES = 128
SUBLANES = 8
VMEM_LIMIT = 56 * 1024 * 1024


def _cparams(*sem):
    return pltpu.CompilerParams(dimension_semantics=sem, vmem_limit_bytes=VMEM_LIMIT)


def _silu(x):
    return x * (1.0 / (1.0 + jnp.exp(-x)))


def _pick(n, cands):
    for c in cands:
        if n % c == 0:
            return c
    raise ValueError(f"no tile for {n}")


def _ada_kernel(c_ref, w_ref, b_ref, o_ref):
    s = _silu(c_ref[...])
    o_ref[0] = jnp.dot(s, w_ref[0], preferred_element_type=F32, precision=HIGHEST) + b_ref[0]


def _ada(c8, ada_w, ada_b):
    depth, d, n = ada_w.shape
    tn = _pick(n, (1536, 1024, 512, 256, 128))
    return pl.pallas_call(
        _ada_kernel,
        grid=(depth, n // tn),
        in_specs=[pl.BlockSpec((SUBLANES, d), lambda l, j: (0, 0)),
                  pl.BlockSpec((1, d, tn), lambda l, j: (l, 0, j)),
                  pl.BlockSpec((1, 1, tn), lambda l, j: (l, 0, j))],
        out_specs=pl.BlockSpec((1, SUBLANES, tn), lambda l, j: (l, 0, j)),
        out_shape=jax.ShapeDtypeStruct((depth, SUBLANES, n), F32),
        compiler_params=_cparams("parallel", "parallel"),
        name="ada",
    )(c8, ada_w, ada_b.reshape(depth, 1, n))


def _mod_rows(mod_ref, r, d, ks):
    return [mod_ref[pl.ds(r, 1), k * d:(k + 1) * d] for k in ks]


def _norm_mod(x, g, shift, scale):
    ms = jnp.sum(x * x, axis=-1, keepdims=True) * (1.0 / x.shape[-1])
    y = x * lax.rsqrt(ms + EPS) * g
    return y * (1.0 + scale) + shift


def _inproj_kernel(x_ref, mod_ref, g_ref, w_ref, qg_ref, kg_ref, cos_ref, sin_ref, *rest, row0, mtpb, d, n_alias):
    rq_ref, rk_ref, rv_ref, rg_ref, dq_ref = rest[n_alias:n_alias + 5]
    kv_refs = rest[n_alias + 5:]
    i = pl.program_id(0)
    r = row0 + i // mtpb
    shift, scale = _mod_rows(mod_ref, r, d, (0, 1))
    h = _norm_mod(x_ref[...], g_ref[...], shift, scale).astype(BF16)
    w = 4 * HEAD_W

    def proj(j):
        return jnp.dot(h, w_ref[:, j * w:(j + 1) * w], preferred_element_type=F32)

    rq_ref[...] = proj(0).astype(BF16)
    rk_ref[...] = (proj(1) * (HEAD_W ** -0.5)).astype(BF16)
    rv_ref[...] = proj(2).astype(BF16)
    rg_ref[...] = proj(3).astype(BF16)

    tm = h.shape[0]
    lane = lax.broadcasted_iota(jnp.int32, (tm, w), 1)
    rowi = lax.broadcasted_iota(jnp.int32, (w, w), 0) // QK_HEAD
    coli = lax.broadcasted_iota(jnp.int32, (w, w), 1) // QK_HEAD
    seg = jnp.where(rowi == coli, 1.0, 0.0).astype(BF16)
    low = (lane % QK_HEAD) < (QK_HEAD // 2)
    cos = jnp.concatenate([cos_ref[...]] * (w // LANES), axis=1)
    sin = jnp.concatenate([sin_ref[...]] * (w // LANES), axis=1)

    def qk_norm_rope(a, gain):
        ss = jnp.dot((a * a).astype(BF16), seg, preferred_element_type=F32)
        y = a * lax.rsqrt(ss * (1.0 / QK_HEAD) + EPS) * gain
        rot = jnp.where(low, pltpu.roll(y, w - QK_HEAD // 2, 1), pltpu.roll(y, QK_HEAD // 2, 1))
        return y * cos + rot * sin

    q = qk_norm_rope(proj(4), qg_ref[...]) * (QK_HEAD ** -0.5 * LOG2E)
    dq_ref[...] = q.astype(BF16)
    kt = qk_norm_rope(proj(5), kg_ref[...]).T
    first = (lax.broadcasted_iota(jnp.int32, (w, tm), 0) % HEAD_W) < QK_HEAD
    k1t = jnp.where(first, kt, 0.0).astype(BF16)
    k2t = jnp.where(first, 0.0, kt).astype(BF16)
    v = proj(6).astype(BF16)
    ones_col = jnp.where(lax.broadcasted_iota(jnp.int32, (tm, HEAD_W), 1) == 0, 1.0, 0.0).astype(BF16)
    pieces = []
    for hd in range(DIFF_HEADS):
        pieces += [v[:, hd * HEAD_W:(hd + 1) * HEAD_W], ones_col]
    vx = jnp.concatenate(pieces, axis=1)
    for s in range(0, len(kv_refs), 3):
        kv_refs[s][...] = k1t
        kv_refs[s + 1][...] = k2t
        kv_refs[s + 2][...] = vx


def _inproj(x2, modl, g1, w_bf, qg, kg, cos, sin, *, row0, mtpb, tpb, tm, nb, t_kv, shared=None, shared_off=0):
    r, d = x2.shape
    w = 4 * HEAD_W
    n_tiles = r // tm
    row = lambda i: (i, 0)
    full = lambda i: (0, 0)
    tab = lambda i: (i % tpb, 0)
    out = jax.ShapeDtypeStruct((r, w), BF16)

    def kv_specs(off):
        kt = pl.BlockSpec((None, w, tm), lambda i: (i // tpb, 0, off + i % tpb))
        return [kt, kt, pl.BlockSpec((None, tm, 2 * w), lambda i: (i // tpb, off + i % tpb, 0))]

    out_specs = [pl.BlockSpec((tm, w), row)] * 5
    out_shape = [out] * 5
    if t_kv:
        kt_shape = jax.ShapeDtypeStruct((nb, w, t_kv), BF16)
        out_specs += kv_specs(0)
        out_shape += [kt_shape, kt_shape, jax.ShapeDtypeStruct((nb, t_kv, 2 * w), BF16)]
    args = [x2, modl, g1, w_bf, qg, kg, cos, sin]
    in_specs = [pl.BlockSpec((tm, d), row),
                pl.BlockSpec(modl.shape, full),
                pl.BlockSpec((1, d), full),
                pl.BlockSpec(w_bf.shape, full),
                pl.BlockSpec((1, w), full),
                pl.BlockSpec((1, w), full),
                pl.BlockSpec((tm, LANES), tab),
                pl.BlockSpec((tm, LANES), tab)]
    aliases = {}
    if shared is not None:
        assert shared_off % tm == 0
        for s in shared:
            aliases[len(args)] = len(out_shape)
            args.append(s)
            in_specs.append(pl.BlockSpec(memory_space=pl.ANY))
            out_shape.append(jax.ShapeDtypeStruct(s.shape, s.dtype))
        out_specs += kv_specs(shared_off // tm)
    return pl.pallas_call(
        functools.partial(_inproj_kernel, row0=row0, mtpb=mtpb, d=d, n_alias=len(aliases)),
        grid=(n_tiles,),
        in_specs=in_specs,
        out_specs=out_specs,
        out_shape=out_shape,
        input_output_aliases=aliases,
        compiler_params=_cparams("parallel"),
        name="inproj",
    )(*args)


def _ret_kernel(lg_ref, q_ref, k_ref, v_ref, g_ref, gn_ref, s0f_ref, s0b_ref,
                o_ref, sf_ref, sb_ref, of_scr, ob_scr, *, c, n_chunks):
    hd = pl.program_id(1)
    lgf = lg_ref[0, hd]
    lgb = lg_ref[1, hd]
    rel = (lax.broadcasted_iota(jnp.int32, (c, c), 0) - lax.broadcasted_iota(jnp.int32, (c, c), 1)).astype(F32)
    dmat_f = jnp.where(rel >= 0, jnp.exp(lgf * jnp.maximum(rel, 0.0)), 0.0)
    dmat_b = jnp.where(rel <= 0, jnp.exp(lgb * jnp.maximum(-rel, 0.0)), 0.0)
    row = lax.broadcasted_iota(jnp.int32, (c, HEAD_W), 0).astype(F32)
    zeta_f = jnp.exp(lgf * (c - 1.0 - row))
    xi_f = jnp.exp(lgf * (row + 1.0))
    zeta_b = jnp.exp(lgb * row)
    xi_b = jnp.exp(lgb * (c - row))
    dec_f = jnp.exp(jnp.full((HEAD_W, HEAD_W), lgf * c, F32))
    dec_b = jnp.exp(jnp.full((HEAD_W, HEAD_W), lgb * c, F32))

    sf_ref[0, 0] = s0f_ref[0, 0]
    sb_ref[0, 0] = s0b_ref[0, 0]

    nt = (((1,), (1,)), ((), ()))

    def one(ci, dmat, zeta, xi, dec, s_ref, o_scr):
        rows = pl.ds(pl.multiple_of(ci * c, c), c)
        q = q_ref[rows, :]
        k = k_ref[rows, :]
        v = v_ref[rows, :]
        s = s_ref[0, 0]
        sc = lax.dot_general(q, k, nt, preferred_element_type=F32) * dmat
        o = jnp.dot(sc.astype(BF16), v, preferred_element_type=F32)
        o = o + xi * jnp.dot(q, s.astype(BF16), preferred_element_type=F32)
        o_scr[rows, :] = o
        kzt = (k.astype(F32) * zeta).T.astype(BF16)
        s_ref[0, 0] = dec * s + jnp.dot(kzt, v, preferred_element_type=F32)

    unroll = _pick(n_chunks, (4, 2, 1))

    def body(i, carry):
        for u in range(unroll):
            ci = i * unroll + u
            one(ci, dmat_f, zeta_f, xi_f, dec_f, sf_ref, of_scr)
            one(n_chunks - 1 - ci, dmat_b, zeta_b, xi_b, dec_b, sb_ref, ob_scr)
        return carry

    lax.fori_loop(0, n_chunks // unroll, body, 0)

    gn = gn_ref[...]
    fr = unroll * c

    def fin(ci, carry):
        rows = pl.ds(pl.multiple_of(ci * fr, fr), fr)
        y = of_scr[rows, :] + ob_scr[rows, :]
        mu = jnp.sum(y, axis=-1, keepdims=True) * (1.0 / HEAD_W)
        yc = y - mu
        var = jnp.sum(yc * yc, axis=-1, keepdims=True) * (1.0 / HEAD_W)
        yn = yc * lax.rsqrt(var + EPS) * gn
        o_ref[rows, :] = (_silu(g_ref[rows, :].astype(F32)) * yn).astype(BF16)
        return carry

    lax.fori_loop(0, n_chunks // unroll, fin, 0)


def _retention(lg, rq, rk, rv, rg, gn, s0f, s0b, *, nb, t):
    c = _pick(t, (256, CHUNK))
    n_chunks = t // c
    blk = pl.BlockSpec((t, HEAD_W), lambda b, h: (b, h))
    sblk = pl.BlockSpec((1, 1, HEAD_W, HEAD_W), lambda b, h: (b, h, 0, 0))
    s_shape = jax.ShapeDtypeStruct((nb, RET_HEADS, HEAD_W, HEAD_W), F32)
    return pl.pallas_call(
        functools.partial(_ret_kernel, c=c, n_chunks=n_chunks),
        grid=(nb, RET_HEADS),
        in_specs=[pl.BlockSpec(memory_space=pltpu.SMEM),
                  blk, blk, blk, blk,
                  pl.BlockSpec((1, HEAD_W), lambda b, h: (0, h)),
                  sblk, sblk],
        out_specs=[blk, sblk, sblk],
        out_shape=[jax.ShapeDtypeStruct(rq.shape, BF16), s_shape, s_shape],
        scratch_shapes=[pltpu.VMEM((t, HEAD_W), F32), pltpu.VMEM((t, HEAD_W), F32)],
        compiler_params=_cparams("parallel", "parallel"),
        name="retention",
    )(lg, rq, rk, rv, rg, gn, s0f, s0b)


def _attn_kernel(lam_ref, q_ref, k1_ref, k2_ref, v_ref, g_ref, o_ref, a1_scr, a2_scr, *, tk, post):
    @pl.when(pl.program_id(0) == 0)
    def _():
        a1_scr[...] = jnp.ones(a1_scr.shape, F32)
        a2_scr[...] = jnp.ones(a2_scr.shape, F32)

    def finish(a_scr):
        a = a_scr[...]
        return a[:, :HEAD_W] * (1.0 / a[:, HEAD_W:HEAD_W + 1])

    o = finish(a1_scr) - lam_ref[0] * finish(a2_scr)
    ms = jnp.sum(o * o, axis=-1, keepdims=True) * (1.0 / HEAD_W)
    o_ref[...] = (o * lax.rsqrt(ms + EPS) * g_ref[...] * post).astype(BF16)

    q = q_ref[...]

    def component(kt_ref):
        m = a = None
        for c0 in range(0, kt_ref.shape[1], tk):
            s = jnp.dot(q, kt_ref[:, c0:c0 + tk], preferred_element_type=F32)
            mc = jnp.max(s, axis=-1, keepdims=True)
            mn = mc if m is None else jnp.maximum(m, mc)
            p = jnp.exp2(s - mn).astype(BF16)
            pv = jnp.dot(p, v_ref[c0:c0 + tk, :], preferred_element_type=F32)
            a = pv if m is None else jnp.exp2(m - mn) * a + pv
            m = mn
        return a

    a1_scr[...] = component(k1_ref)
    a2_scr[...] = component(k2_ref)


def _attention(lam, q, k1, k2, vx, g, *, nb, tq_total, tk_total, post):
    tq = _pick(tq_total, (512, 256, 128))
    nq = tq_total // tq
    tk = _pick(tk_total, (256, 128))
    n = nb * DIFF_HEADS * nq

    def tile(t):
        return t // (DIFF_HEADS * nq), (t // nq) % DIFF_HEADS, t % nq

    def at(f, lag):
        return lambda s: f(*tile(jnp.maximum(s - 1, 0) if lag else jnp.minimum(s, n - 1)))

    qmap = lambda b, h, i: (b * nq + i, h)
    return pl.pallas_call(
        functools.partial(_attn_kernel, tk=tk, post=post),
        grid=(n + 1,),
        in_specs=[pl.BlockSpec(memory_space=pltpu.SMEM),
                  pl.BlockSpec((tq, HEAD_W), at(qmap, False)),
                  pl.BlockSpec((None, HEAD_W, tk_total), at(lambda b, h, i: (b, h, 0), False)),
                  pl.BlockSpec((None, HEAD_W, tk_total), at(lambda b, h, i: (b, h, 0), False)),
                  pl.BlockSpec((None, tk_total, 2 * HEAD_W), at(lambda b, h, i: (b, 0, h), False)),
                  pl.BlockSpec((1, HEAD_W), lambda s: (0, 0))],
        out_specs=pl.BlockSpec((tq, HEAD_W), at(qmap, True)),
        out_shape=jax.ShapeDtypeStruct(q.shape, BF16),
        scratch_shapes=[pltpu.VMEM((tq, 2 * HEAD_W), F32), pltpu.VMEM((tq, 2 * HEAD_W), F32)],
        compiler_params=_cparams("arbitrary"),
        name="attention",
    )(lam, q, k1, k2, vx, g)


def _mixed_residual(ret_ref, dif_ref, w_ref, x_ref, gate):
    half = w_ref.shape[0] // 2
    mix = (jnp.dot(ret_ref[...], w_ref[:half, :], preferred_element_type=F32)
           + jnp.dot(dif_ref[...], w_ref[half:, :], preferred_element_type=F32))
    return x_ref[...] + gate * mix


def _outproj_kernel(ret_ref, dif_ref, w_ref, x_ref, mod_ref, xo_ref, *, tpb, d):
    (gate,) = _mod_rows(mod_ref, pl.program_id(0) // tpb, d, (2,))
    xo_ref[...] = _mixed_residual(ret_ref, dif_ref, w_ref, x_ref, gate)


def _outproj(ret, dif, w_bf, x2, modl, *, tpb, tm):
    r, d = x2.shape
    row = lambda i: (i, 0)
    full = lambda i: (0, 0)
    return pl.pallas_call(
        functools.partial(_outproj_kernel, tpb=tpb, d=d),
        grid=(r // tm,),
        in_specs=[pl.BlockSpec((tm, ret.shape[1]), row),
                  pl.BlockSpec((tm, dif.shape[1]), row),
                  pl.BlockSpec(w_bf.shape, full),
                  pl.BlockSpec((tm, d), row),
                  pl.BlockSpec(modl.shape, full)],
        out_specs=pl.BlockSpec((tm, d), row),
        out_shape=jax.ShapeDtypeStruct((r, d), F32),
        compiler_params=_cparams("parallel"),
        name="outproj",
    )(ret, dif, w_bf, x2, modl)


def _swiglu_tile(h, wg_ref, wu_ref, wd_ref, fc):
    ff = wg_ref.shape[-1]
    acc = None
    for c0 in range(0, ff, fc):
        g = jnp.dot(h, wg_ref[0, :, c0:c0 + fc], preferred_element_type=F32)
        u = jnp.dot(h, wu_ref[0, :, c0:c0 + fc], preferred_element_type=F32)
        a = (_silu(g) * u).astype(BF16)
        part = jnp.dot(a, wd_ref[0, c0:c0 + fc, :], preferred_element_type=F32)
        acc = part if acc is None else acc + part
    return acc


def _mix_ffn_kernel(ret_ref, dif_ref, wo_ref, x_ref, mod_ref, g_ref, wg_ref, wu_ref, wd_ref, o_ref,
                    *, row0, tpb, d, fc):
    r = row0 + pl.program_id(0) // tpb
    gate1, shift, scale, gate2 = _mod_rows(mod_ref, r, d, (2, 3, 4, 5))
    x = _mixed_residual(ret_ref, dif_ref, wo_ref, x_ref, gate1)
    h = _norm_mod(x, g_ref[...], shift, scale).astype(BF16)
    o_ref[...] = x + gate2 * _swiglu_tile(h, wg_ref, wu_ref, wd_ref, fc)


def _ff_chunk(ff):
    return _pick(ff, (256, 128))


def _mix_ffn(ret, dif, wo_bf, x2, modl, g2, wg, wu, wd, *, row0, tpb, tm):
    r, d = x2.shape
    ff = wg.shape[-1]
    row = lambda i: (i, 0)
    full = lambda i: (0, 0)
    wfull = lambda i: (0, 0, 0)
    return pl.pallas_call(
        functools.partial(_mix_ffn_kernel, row0=row0, tpb=tpb, d=d, fc=_ff_chunk(ff)),
        grid=(r // tm,),
        in_specs=[pl.BlockSpec((tm, ret.shape[1]), row),
                  pl.BlockSpec((tm, dif.shape[1]), row),
                  pl.BlockSpec(wo_bf.shape, full),
                  pl.BlockSpec((tm, d), row),
                  pl.BlockSpec(modl.shape, full),
                  pl.BlockSpec((1, d), full),
                  pl.BlockSpec((1, d, ff), wfull),
                  pl.BlockSpec((1, d, ff), wfull),
                  pl.BlockSpec((1, ff, d), wfull)],
        out_specs=pl.BlockSpec((tm, d), row),
        out_shape=jax.ShapeDtypeStruct((r, d), F32),
        compiler_params=_cparams("parallel"),
        name="mix_ffn",
    )(ret, dif, wo_bf, x2, modl, g2, wg, wu, wd)


def _rows_to_wide(ref, tm, lead=()):
    return jnp.concatenate([ref[lead + (pl.ds(k, tm, stride=SUBLANES), slice(None))] for k in range(SUBLANES)],
                           axis=1)


def _wide_to_rows(ref, val):
    tm = val.shape[0]
    for k in range(SUBLANES):
        ref[pl.ds(k, tm, stride=SUBLANES), :] = val[:, k * LANES:(k + 1) * LANES]


def _row_tile(ref, r):
    return ref.at[pl.ds(pl.multiple_of(r * SUBLANES, SUBLANES), SUBLANES), :]


ROW_DMA_UNROLL = 16


def _start_rows(n, copy):
    def body(i, carry):
        for u in range(ROW_DMA_UNROLL):
            copy(i * ROW_DMA_UNROLL + u).start(priority=u % 2)
        return carry
    lax.fori_loop(0, n // ROW_DMA_UNROLL, body, 0)


def _wait_rows(n, copy):
    def body(i, carry):
        for u in range(ROW_DMA_UNROLL):
            copy(i * ROW_DMA_UNROLL + u).wait()
        return carry
    lax.fori_loop(0, n // ROW_DMA_UNROLL, body, 0)


def _ffn_group_kernel(te_ref, nu_ref, h_ref, wg_ref, wu_ref, wd_ref, o_ref, *, tm, fc):
    i = pl.program_id(0)

    @pl.when(i < nu_ref[0])
    def _():
        _wide_to_rows(o_ref, _swiglu_tile(_rows_to_wide(h_ref, tm).astype(BF16), wg_ref, wu_ref, wd_ref, fc))

    @pl.when(i >= nu_ref[0])
    def _():
        o_ref[...] = jnp.zeros(o_ref.shape, F32)


def _ffn_grouped(tile_expert, n_used, hs, wg, wu, wd, *, tm):
    d, ff = wg.shape[-2:]
    assert d == SUBLANES * LANES
    rows = hs.shape[0] // SUBLANES
    row = lambda i, te, nu: (i, 0)
    wsel = lambda i, te, nu: (te[i], 0, 0)
    return pl.pallas_call(
        functools.partial(_ffn_group_kernel, tm=tm, fc=_ff_chunk(ff)),
        grid_spec=pltpu.PrefetchScalarGridSpec(
            num_scalar_prefetch=2,
            grid=(rows // tm,),
            in_specs=[pl.BlockSpec((tm * SUBLANES, LANES), row),
                      pl.BlockSpec((1, d, ff), wsel),
                      pl.BlockSpec((1, d, ff), wsel),
                      pl.BlockSpec((1, ff, d), wsel)],
            out_specs=pl.BlockSpec((tm * SUBLANES, LANES), row)),
        out_shape=jax.ShapeDtypeStruct(hs.shape, F32),
        compiler_params=_cparams("arbitrary"),
        name="ffn_grouped",
    )(tile_expert, n_used, hs, wg, wu, wd)


def _route_kernel(x_ref, mod_ref, g_ref, rw_ref, rb_ref, ri_ref, rwt_ref, cnt_ref, carry, *, tpb, d, ne):
    i = pl.program_id(0)

    @pl.when(i == 0)
    def _():
        carry[...] = jnp.zeros(carry.shape, F32)

    r = i // tpb
    shift, scale = _mod_rows(mod_ref, r, d, (3, 4))
    h = _norm_mod(x_ref[...], g_ref[...], shift, scale)
    tm = h.shape[0]
    nt = (((1,), (1,)), ((), ()))
    logits = lax.dot_general(rw_ref[...], h, nt, preferred_element_type=F32, precision=HIGHEST)
    logits = logits + rb_ref[...]
    eid = lax.broadcasted_iota(jnp.int32, (ne, tm), 0)
    m1 = jnp.max(logits, axis=0, keepdims=True)
    i1 = jnp.min(jnp.where(logits == m1, eid, ne), axis=0, keepdims=True)
    rest = jnp.where(eid == i1, -jnp.inf, logits)
    m2 = jnp.max(rest, axis=0, keepdims=True)
    i2 = jnp.min(jnp.where(rest == m2, eid, ne), axis=0, keepdims=True)
    e2 = jnp.exp(m2 - m1)
    w1 = 1.0 / (1.0 + e2)
    w2 = e2 * w1
    sel1 = eid == i1
    sel2 = eid == i2
    member = jnp.where(sel1, 1.0, jnp.where(sel2, 1.0, 0.0))
    tr = lax.broadcasted_iota(jnp.int32, (tm, tm), 0)
    tc = lax.broadcasted_iota(jnp.int32, (tm, tm), 1)
    before = jnp.where(tr < tc, 1.0, 0.0).astype(BF16)
    prefix = jnp.dot(member.astype(BF16), before, preferred_element_type=F32) + carry[:, 0:1]
    rank1 = jnp.sum(jnp.where(sel1, prefix, 0.0), axis=0, keepdims=True)
    rank2 = jnp.sum(jnp.where(sel2, prefix, 0.0), axis=0, keepdims=True)
    zi = jnp.zeros((SUBLANES - 4, tm), jnp.int32)
    ri_ref[...] = jnp.concatenate([i1, i2, rank1.astype(jnp.int32), rank2.astype(jnp.int32), zi], axis=0)
    rwt_ref[...] = jnp.concatenate([w1, w2, jnp.zeros((SUBLANES - 2, tm), F32)], axis=0)
    carry[...] = carry[...] + jnp.sum(member, axis=1, keepdims=True)
    cnt_ref[...] = carry[...]


def _route(x2, modl, g2, rw_t, rb, *, tpb, tm):
    n, d = x2.shape
    ne = rw_t.shape[0]
    return pl.pallas_call(
        functools.partial(_route_kernel, tpb=tpb, d=d, ne=ne),
        grid=(n // tm,),
        in_specs=[pl.BlockSpec((tm, d), lambda i: (i, 0)),
                  pl.BlockSpec(modl.shape, lambda i: (0, 0)),
                  pl.BlockSpec((1, d), lambda i: (0, 0)),
                  pl.BlockSpec((ne, d), lambda i: (0, 0)),
                  pl.BlockSpec((ne, 1), lambda i: (0, 0))],
        out_specs=[pl.BlockSpec((SUBLANES, tm), lambda i: (0, i)),
                   pl.BlockSpec((SUBLANES, tm), lambda i: (0, i)),
                   pl.BlockSpec((ne, LANES), lambda i: (0, 0))],
        out_shape=[jax.ShapeDtypeStruct((SUBLANES, n), jnp.int32),
                   jax.ShapeDtypeStruct((SUBLANES, n), F32),
                   jax.ShapeDtypeStruct((ne, LANES), F32)],
        scratch_shapes=[pltpu.VMEM((ne, LANES), F32)],
        compiler_params=_cparams("arbitrary"),
        name="route",
    )(x2, modl, g2, rw_t, rb)


def _pos_kernel(off_ref, ri_ref, pos_ref, *, ne):
    ri = ri_ref[...]
    e = ri[0:2, :]
    pos = ri[2:4, :]
    for k in range(ne):
        pos = pos + jnp.where(e == k, off_ref[k], 0)
    pos_ref[...] = pos


def _positions(offsets, route_i, *, ne):
    n = route_i.shape[1]
    return pl.pallas_call(
        functools.partial(_pos_kernel, ne=ne),
        in_specs=[pl.BlockSpec(memory_space=pltpu.SMEM),
                  pl.BlockSpec(route_i.shape, lambda: (0, 0))],
        out_specs=pl.BlockSpec((TOP_K, n), lambda: (0, 0)),
        out_shape=jax.ShapeDtypeStruct((TOP_K, n), jnp.int32),
        name="positions",
    )(offsets, route_i)


def _dispatch_kernel(pos_ref, x_ref, mod_ref, g_ref, xs_in_ref, xs_ref, h_scr, sem, *, tpb, d, n):
    del xs_in_ref
    i = pl.program_id(0)
    tm = x_ref.shape[0]
    shift, scale = _mod_rows(mod_ref, i // tpb, d, (3, 4))
    _wide_to_rows(h_scr, _norm_mod(x_ref[...], g_ref[...], shift, scale))

    def copy(j, t):
        dst = pos_ref[j * n + i * tm + t]
        return pltpu.make_async_copy(_row_tile(h_scr, t), _row_tile(xs_ref, dst), sem)

    for j in range(TOP_K):
        _start_rows(tm, functools.partial(copy, j))
    for j in range(TOP_K):
        _wait_rows(tm, functools.partial(copy, j))


def _dispatch(pos_flat, x2, modl, g2, xs_init, *, tpb, tm):
    n, d = x2.shape
    return pl.pallas_call(
        functools.partial(_dispatch_kernel, tpb=tpb, d=d, n=n),
        grid_spec=pltpu.PrefetchScalarGridSpec(
            num_scalar_prefetch=1,
            grid=(n // tm,),
            in_specs=[pl.BlockSpec((tm, d), lambda i, pos: (i, 0)),
                      pl.BlockSpec(modl.shape, lambda i, pos: (0, 0)),
                      pl.BlockSpec((1, d), lambda i, pos: (0, 0)),
                      pl.BlockSpec(memory_space=pl.ANY)],
            out_specs=pl.BlockSpec(memory_space=pl.ANY),
            scratch_shapes=[pltpu.VMEM((tm * SUBLANES, LANES), F32), pltpu.SemaphoreType.DMA(())]),
        out_shape=jax.ShapeDtypeStruct(xs_init.shape, F32),
        input_output_aliases={4: 0},
        compiler_params=pltpu.CompilerParams(dimension_semantics=("arbitrary",), vmem_limit_bytes=VMEM_LIMIT,
                                             has_side_effects=True),
        name="dispatch",
    )(pos_flat, x2, modl, g2, xs_init)


def _combine_kernel(pos_ref, ys_ref, w_ref, x_ref, mod_ref, o_ref, ybuf, sem, *, tpb, d, n, n_steps):
    i = pl.program_id(0)
    tm = x_ref.shape[0]

    def copy(step, slot, j, t):
        src = pos_ref[j * n + step * tm + t]
        return pltpu.make_async_copy(_row_tile(ys_ref, src), _row_tile(ybuf.at[slot, j], t), sem.at[slot])

    def fetch(step, slot):
        for j in range(TOP_K):
            _start_rows(tm, functools.partial(copy, step, slot, j))

    @pl.when(i == 0)
    def _():
        fetch(0, 0)

    @pl.when(i + 1 < n_steps)
    def _():
        fetch(i + 1, (i + 1) % 2)

    slot = i % 2
    for j in range(TOP_K):
        _wait_rows(tm, functools.partial(copy, i, slot, j))

    (gate,) = _mod_rows(mod_ref, i // tpb, d, (5,))
    w = w_ref[...]
    y = w[:, 0:1] * _rows_to_wide(ybuf, tm, (slot, 0)) + w[:, 1:2] * _rows_to_wide(ybuf, tm, (slot, 1))
    o_ref[...] = x_ref[...] + gate * y


def _combine(pos_flat, ys, w_t, x2, modl, *, tpb, tm):
    n, d = x2.shape
    n_steps = n // tm
    row = lambda i, pos: (i, 0)
    return pl.pallas_call(
        functools.partial(_combine_kernel, tpb=tpb, d=d, n=n, n_steps=n_steps),
        grid_spec=pltpu.PrefetchScalarGridSpec(
            num_scalar_prefetch=1,
            grid=(n_steps,),
            in_specs=[pl.BlockSpec(memory_space=pl.ANY),
                      pl.BlockSpec((tm, TOP_K), row),
                      pl.BlockSpec((tm, d), row),
                      pl.BlockSpec(modl.shape, lambda i, pos: (0, 0))],
            out_specs=pl.BlockSpec((tm, d), row),
            scratch_shapes=[pltpu.VMEM((2, TOP_K, tm * SUBLANES, LANES), F32), pltpu.SemaphoreType.DMA((2,))]),
        out_shape=jax.ShapeDtypeStruct((n, d), F32),
        compiler_params=pltpu.CompilerParams(dimension_semantics=("arbitrary",), vmem_limit_bytes=VMEM_LIMIT),
        name="combine",
    )(pos_flat, ys, w_t, x2, modl)


def _rope_tables(t):
    n_freq = QK_HEAD // 4
    rows = t // GRID_W
    row_pos = jnp.repeat(jnp.arange(rows, dtype=F32), GRID_W)
    col_pos = jnp.tile(jnp.arange(GRID_W, dtype=F32), rows)
    inv_freq = ROPE_BASE ** (-jnp.arange(n_freq, dtype=F32) / n_freq)
    ang = jnp.concatenate([row_pos[:, None] * inv_freq, col_pos[:, None] * inv_freq], axis=-1)
    ang = jnp.concatenate([ang, ang], axis=-1)
    sign = jnp.where(jnp.arange(QK_HEAD) < QK_HEAD // 2, -1.0, 1.0).astype(F32)
    rep = LANES // QK_HEAD
    return jnp.tile(jnp.cos(ang), (1, rep)), jnp.tile(jnp.sin(ang) * sign, (1, rep))


def _moe_plan(counts, n_tiles, tm):
    tiles = (counts + tm - 1) // tm
    ends = jnp.cumsum(tiles)
    offsets = (ends - tiles) * tm
    n_used = ends[-1]
    ids = jnp.arange(n_tiles, dtype=jnp.int32)
    te = jnp.sum(jnp.minimum(ids, n_used - 1)[:, None] >= ends[None, :], axis=1).astype(jnp.int32)
    return offsets.astype(jnp.int32), te, n_used.reshape(1).astype(jnp.int32)


def kernel(x, c, ctx, c_ctx, ada_w, ada_b, norm1_g, norm2_g, w_in, ret_decay_fwd, ret_decay_bwd, ret_gn_g, diff_qn_g, diff_kn_g, lam_q1, lam_k1, lam_q2, lam_k2, diff_subln_g, w_out, ffn_w_gate, ffn_w_up, ffn_w_down, router_w, router_b, moe_w_gate, moe_w_up, moe_w_down):
    b, t, d = x.shape
    tc = ctx.shape[1]
    depth = ada_w.shape[0]
    ne = router_w.shape[-1]
    w = 4 * HEAD_W
    assert b + 1 <= SUBLANES and t % GRID_W == 0 and t % CHUNK == 0 and tc % CHUNK == 0
    assert w_in.shape[-1] == 7 * w and d % LANES == 0

    c8 = jnp.zeros((SUBLANES, d), F32).at[:b].set(c).at[b].set(c_ctx)
    mods = _ada(c8, ada_w, ada_b)

    cos, sin = _rope_tables(t)
    cos_c = jnp.ones((b * tc, LANES), F32)
    sin_c = jnp.zeros((b * tc, LANES), F32)
    tm = _pick(t, (512, 256, 128))
    tmc = _pick(tc, (256, 128))
    tpb = t // tm
    ntc = (b * tc) // tmc
    assert t % tmc == 0
    rep4 = lambda g: jnp.tile(g.astype(F32), w // g.shape[0]).reshape(1, w)

    xl = x.reshape(b * t, d)
    xc = ctx.reshape(b * tc, d)
    zero_state = jnp.zeros((b, RET_HEADS, HEAD_W, HEAD_W), F32)

    for l in range(depth):
        ctx_out = l < depth - 1
        lam_init = 0.8 - 0.6 * math.exp(-0.3 * l)
        modl = mods[l]
        g1 = norm1_g[l].reshape(1, d)
        g2 = norm2_g[l].reshape(1, d)
        w_bf = w_in[l].astype(BF16)
        wo_bf = w_out[l].astype(BF16)
        qg, kg = rep4(diff_qn_g[l]), rep4(diff_kn_g[l])
        gn = ret_gn_g[l].reshape(1, w)
        sg = diff_subln_g[l].reshape(1, HEAD_W)
        lg = jnp.stack([jax.nn.log_sigmoid(ret_decay_fwd[l].astype(F32)),
                        jax.nn.log_sigmoid(ret_decay_bwd[l].astype(F32))])
        lam = (jnp.exp(jnp.sum(lam_q1[l].astype(F32) * lam_k1[l].astype(F32)))
               - jnp.exp(jnp.sum(lam_q2[l].astype(F32) * lam_k2[l].astype(F32))) + lam_init).reshape(1)
        post = 1.0 - lam_init

        rq, rk, rv, rg, dq, dk1, dk2, dv = _inproj(
            xl, modl, g1, w_bf, qg, kg, cos, sin, row0=0, mtpb=tpb, tpb=tpb, tm=tm, nb=b, t_kv=0,
            shared=(jnp.zeros((b, w, t + tc), BF16), jnp.zeros((b, w, t + tc), BF16),
                    jnp.zeros((b, t + tc, 2 * w), BF16)))
        rqc, rkc, rvc, rgc, dqc, dk1c, dk2c, dvc, dk1, dk2, dv = _inproj(
            xc, modl, g1, w_bf, qg, kg, cos_c, sin_c, row0=b, mtpb=ntc, tpb=tc // tmc, tm=tmc, nb=b, t_kv=tc,
            shared=(dk1, dk2, dv), shared_off=t)

        ret_c, sc_f, sc_b = _retention(lg, rqc, rkc, rvc, rgc, gn, zero_state, zero_state, nb=b, t=tc)
        ret, _, _ = _retention(lg, rq, rk, rv, rg, gn, sc_f, sc_b, nb=b, t=t)

        dif = _attention(lam, dq, dk1, dk2, dv, sg, nb=b, tq_total=t, tk_total=tc + t, post=post)
        if ctx_out:
            dif_c = _attention(lam, dqc, dk1c, dk2c, dvc, sg, nb=b, tq_total=tc, tk_total=tc, post=post)

        if l % 2 == 0:
            j = l // 2
            wg = ffn_w_gate[j:j + 1].astype(BF16)
            wu = ffn_w_up[j:j + 1].astype(BF16)
            wd = ffn_w_down[j:j + 1].astype(BF16)
            xl = _mix_ffn(ret, dif, wo_bf, xl, modl, g2, wg, wu, wd, row0=0, tpb=tpb, tm=tm)
            if ctx_out:
                xc = _mix_ffn(ret_c, dif_c, wo_bf, xc, modl, g2, wg, wu, wd, row0=b, tpb=ntc, tm=tmc)
        else:
            if ctx_out:
                raise NotImplementedError("routed channel mixer on context tokens")
            j = l // 2
            n = b * t
            tme = _pick(t, (512, 256, 128))
            n_tiles = (n * TOP_K) // tme + ne
            xl = _outproj(ret, dif, wo_bf, xl, modl, tpb=tpb, tm=tm)
            route_i, route_w, counts = _route(xl, modl, g2, router_w[j].T.astype(F32),
                                              router_b[j].reshape(ne, 1).astype(F32), tpb=tpb, tm=tm)
            offsets, tile_expert, n_used = _moe_plan(counts[:, 0].astype(jnp.int32), n_tiles, tme)
            pos = _positions(offsets, route_i, ne=ne)
            pos_flat = pos.reshape(TOP_K * n)
            xs = _dispatch(pos_flat, xl, modl, g2, jnp.zeros((n_tiles * tme * SUBLANES, LANES), F32),
                           tpb=t // tme, tm=tme)
            ys = _ffn_grouped(tile_expert, n_used, xs,
                              moe_w_gate[j].astype(BF16), moe_w_up[j].astype(BF16),
                              moe_w_down[j].astype(BF16), tm=tme)
            xl = _combine(pos_flat, ys, route_w[:TOP_K].T, xl, modl, tpb=t // tme, tm=tme)
    return xl.reshape(b, t, d)
```

```python
import functools
import math

import jax
import jax.numpy as jnp
from jax import lax
from jax.experimental import pallas as pl
from jax.experimental.pallas import tpu as pltpu

F32 = jnp.float32
BF16 = jnp.bfloat16
HIGHEST = lax.Precision.HIGHEST

EPS = 1e-6
LOG2E = math.log2(math.e)
ROPE_BASE = 10000.0
GRID_W = 64
RET_HEADS = 4
DIFF_HEADS = 4
HEAD_W = 128
QK_HEAD = 64
CHUNK = 128
TOP_K = 2
LANES = 128
SUBLANES = 8
VMEM_LIMIT = 56 * 1024 * 1024


def _cparams(*sem):
    return pltpu.CompilerParams(dimension_semantics=sem, vmem_limit_bytes=VMEM_LIMIT)


def _silu(x):
    return x * (1.0 / (1.0 + jnp.exp(-x)))


def _pick(n, cands):
    for c in cands:
        if n % c == 0:
            return c
    raise ValueError(f"no tile for {n}")


def _ada_kernel(c_ref, w_ref, b_ref, o_ref):
    s = _silu(c_ref[...])
    o_ref[0] = jnp.dot(s, w_ref[0], preferred_element_type=F32, precision=HIGHEST) + b_ref[0]


def _ada(c8, ada_w, ada_b):
    depth, d, n = ada_w.shape
    tn = _pick(n, (1536, 1024, 512, 256, 128))
    return pl.pallas_call(
        _ada_kernel,
        grid=(depth, n // tn),
        in_specs=[pl.BlockSpec((SUBLANES, d), lambda l, j: (0, 0)),
                  pl.BlockSpec((1, d, tn), lambda l, j: (l, 0, j)),
                  pl.BlockSpec((1, 1, tn), lambda l, j: (l, 0, j))],
        out_specs=pl.BlockSpec((1, SUBLANES, tn), lambda l, j: (l, 0, j)),
        out_shape=jax.ShapeDtypeStruct((depth, SUBLANES, n), F32),
        compiler_params=_cparams("parallel", "parallel"),
        name="ada",
    )(c8, ada_w, ada_b.reshape(depth, 1, n))


def _mod_rows(mod_ref, r, d, ks):
    return [mod_ref[pl.ds(r, 1), k * d:(k + 1) * d] for k in ks]


def _norm_mod(x, g, shift, scale):
    ms = jnp.sum(x * x, axis=-1, keepdims=True) * (1.0 / x.shape[-1])
    y = x * lax.rsqrt(ms + EPS) * g
    return y * (1.0 + scale) + shift


def _inproj_kernel(x_ref, mod_ref, g_ref, w_ref, qg_ref, kg_ref, cos_ref, sin_ref,
                   rq_ref, rk_ref, rv_ref, rg_ref, dq_ref, dk1_ref, dk2_ref, dv_ref, *, row0, mtpb, d):
    i = pl.program_id(0)
    r = row0 + i // mtpb
    shift, scale = _mod_rows(mod_ref, r, d, (0, 1))
    h = _norm_mod(x_ref[...], g_ref[...], shift, scale).astype(BF16)
    w = 4 * HEAD_W

    def proj(j):
        return jnp.dot(h, w_ref[:, j * w:(j + 1) * w], preferred_element_type=F32)

    rq_ref[...] = proj(0).astype(BF16)
    rk_ref[...] = (proj(1) * (HEAD_W ** -0.5)).astype(BF16)
    rv_ref[...] = proj(2).astype(BF16)
    rg_ref[...] = proj(3).astype(BF16)

    tm = h.shape[0]
    lane = lax.broadcasted_iota(jnp.int32, (tm, w), 1)
    rowi = lax.broadcasted_iota(jnp.int32, (w, w), 0) // QK_HEAD
    coli = lax.broadcasted_iota(jnp.int32, (w, w), 1) // QK_HEAD
    seg = jnp.where(rowi == coli, 1.0, 0.0).astype(BF16)
    low = (lane % QK_HEAD) < (QK_HEAD // 2)
    cos = jnp.concatenate([cos_ref[...]] * (w // LANES), axis=1)
    sin = jnp.concatenate([sin_ref[...]] * (w // LANES), axis=1)

    def qk_norm_rope(a, gain):
        ss = jnp.dot((a * a).astype(BF16), seg, preferred_element_type=F32)
        y = a * lax.rsqrt(ss * (1.0 / QK_HEAD) + EPS) * gain
        rot = jnp.where(low, pltpu.roll(y, w - QK_HEAD // 2, 1), pltpu.roll(y, QK_HEAD // 2, 1))
        return y * cos + rot * sin

    q = qk_norm_rope(proj(4), qg_ref[...]) * (QK_HEAD ** -0.5 * LOG2E)
    dq_ref[...] = q.astype(BF16)
    kt = qk_norm_rope(proj(5), kg_ref[...]).T
    first = (lax.broadcasted_iota(jnp.int32, (w, tm), 0) % HEAD_W) < QK_HEAD
    k1t = jnp.where(first, kt, 0.0).astype(BF16)
    k2t = jnp.where(first, 0.0, kt).astype(BF16)
    v = proj(6).astype(BF16)
    ones_col = jnp.where(lax.broadcasted_iota(jnp.int32, (tm, HEAD_W), 1) == 0, 1.0, 0.0).astype(BF16)
    pieces = []
    for hd in range(DIFF_HEADS):
        pieces += [v[:, hd * HEAD_W:(hd + 1) * HEAD_W], ones_col]
    dk1_ref[...] = k1t
    dk2_ref[...] = k2t
    dv_ref[...] = jnp.concatenate(pieces, axis=1)


def _inproj(x2, modl, g1, w_bf, qg, kg, cos, sin, *, row0, mtpb, tpb, tm):
    r, d = x2.shape
    w = 4 * HEAD_W
    n_tiles = r // tm
    nb, t_kv = n_tiles // tpb, tpb * tm
    row = lambda i: (i, 0)
    full = lambda i: (0, 0)
    tab = lambda i: (i % tpb, 0)
    out = jax.ShapeDtypeStruct((r, w), BF16)
    kt_spec = pl.BlockSpec((None, w, tm), lambda i: (i // tpb, 0, i % tpb))
    kt_shape = jax.ShapeDtypeStruct((nb, w, t_kv), BF16)
    return pl.pallas_call(
        functools.partial(_inproj_kernel, row0=row0, mtpb=mtpb, d=d),
        grid=(n_tiles,),
        in_specs=[pl.BlockSpec((tm, d), row),
                  pl.BlockSpec(modl.shape, full),
                  pl.BlockSpec((1, d), full),
                  pl.BlockSpec(w_bf.shape, full),
                  pl.BlockSpec((1, w), full),
                  pl.BlockSpec((1, w), full),
                  pl.BlockSpec((tm, LANES), tab),
                  pl.BlockSpec((tm, LANES), tab)],
        out_specs=[pl.BlockSpec((tm, w), row)] * 5
        + [kt_spec, kt_spec, pl.BlockSpec((None, tm, 2 * w), lambda i: (i // tpb, i % tpb, 0))],
        out_shape=[out] * 5 + [kt_shape, kt_shape, jax.ShapeDtypeStruct((nb, t_kv, 2 * w), BF16)],
        compiler_params=_cparams("parallel"),
        name="inproj",
    )(x2, modl, g1, w_bf, qg, kg, cos, sin)


def _ret_kernel(lg_ref, q_ref, k_ref, v_ref, g_ref, gn_ref, s0f_ref, s0b_ref,
                o_ref, sf_ref, sb_ref, of_scr, ob_scr, *, c, n_chunks):
    hd = pl.program_id(1)
    lgf = lg_ref[0, hd]
    lgb = lg_ref[1, hd]
    rel = (lax.broadcasted_iota(jnp.int32, (c, c), 0) - lax.broadcasted_iota(jnp.int32, (c, c), 1)).astype(F32)
    dmat_f = jnp.where(rel >= 0, jnp.exp(lgf * jnp.maximum(rel, 0.0)), 0.0)
    dmat_b = jnp.where(rel <= 0, jnp.exp(lgb * jnp.maximum(-rel, 0.0)), 0.0)
    row = lax.broadcasted_iota(jnp.int32, (c, HEAD_W), 0).astype(F32)
    zeta_f = jnp.exp(lgf * (c - 1.0 - row))
    xi_f = jnp.exp(lgf * (row + 1.0))
    zeta_b = jnp.exp(lgb * row)
    xi_b = jnp.exp(lgb * (c - row))
    dec_f = jnp.exp(jnp.full((HEAD_W, HEAD_W), lgf * c, F32))
    dec_b = jnp.exp(jnp.full((HEAD_W, HEAD_W), lgb * c, F32))

    sf_ref[0, 0] = s0f_ref[0, 0]
    sb_ref[0, 0] = s0b_ref[0, 0]

    nt = (((1,), (1,)), ((), ()))

    def one(ci, dmat, zeta, xi, dec, s_ref, o_scr):
        rows = pl.ds(pl.multiple_of(ci * c, c), c)
        q = q_ref[rows, :]
        k = k_ref[rows, :]
        v = v_ref[rows, :]
        s = s_ref[0, 0]
        sc = lax.dot_general(q, k, nt, preferred_element_type=F32) * dmat
        o = jnp.dot(sc.astype(BF16), v, preferred_element_type=F32)
        o = o + xi * jnp.dot(q, s.astype(BF16), preferred_element_type=F32)
        o_scr[rows, :] = o
        kzt = (k.astype(F32) * zeta).T.astype(BF16)
        s_ref[0, 0] = dec * s + jnp.dot(kzt, v, preferred_element_type=F32)

    unroll = _pick(n_chunks, (4, 2, 1))

    def body(i, carry):
        for u in range(unroll):
            ci = i * unroll + u
            one(ci, dmat_f, zeta_f, xi_f, dec_f, sf_ref, of_scr)
            one(n_chunks - 1 - ci, dmat_b, zeta_b, xi_b, dec_b, sb_ref, ob_scr)
        return carry

    lax.fori_loop(0, n_chunks // unroll, body, 0)

    gn = gn_ref[...]
    fr = unroll * c

    def fin(ci, carry):
        rows = pl.ds(pl.multiple_of(ci * fr, fr), fr)
        y = of_scr[rows, :] + ob_scr[rows, :]
        mu = jnp.sum(y, axis=-1, keepdims=True) * (1.0 / HEAD_W)
        yc = y - mu
        var = jnp.sum(yc * yc, axis=-1, keepdims=True) * (1.0 / HEAD_W)
        yn = yc * lax.rsqrt(var + EPS) * gn
        o_ref[rows, :] = (_silu(g_ref[rows, :].astype(F32)) * yn).astype(BF16)
        return carry

    lax.fori_loop(0, n_chunks // unroll, fin, 0)


def _retention(lg, rq, rk, rv, rg, gn, s0f, s0b, *, nb, t):
    c = _pick(t, (256, CHUNK))
    n_chunks = t // c
    blk = pl.BlockSpec((t, HEAD_W), lambda b, h: (b, h))
    sblk = pl.BlockSpec((1, 1, HEAD_W, HEAD_W), lambda b, h: (b, h, 0, 0))
    s_shape = jax.ShapeDtypeStruct((nb, RET_HEADS, HEAD_W, HEAD_W), F32)
    return pl.pallas_call(
        functools.partial(_ret_kernel, c=c, n_chunks=n_chunks),
        grid=(nb, RET_HEADS),
        in_specs=[pl.BlockSpec(memory_space=pltpu.SMEM),
                  blk, blk, blk, blk,
                  pl.BlockSpec((1, HEAD_W), lambda b, h: (0, h)),
                  sblk, sblk],
        out_specs=[blk, sblk, sblk],
        out_shape=[jax.ShapeDtypeStruct(rq.shape, BF16), s_shape, s_shape],
        scratch_shapes=[pltpu.VMEM((t, HEAD_W), F32), pltpu.VMEM((t, HEAD_W), F32)],
        compiler_params=_cparams("parallel", "parallel"),
        name="retention",
    )(lg, rq, rk, rv, rg, gn, s0f, s0b)


def _attn_kernel(lam_ref, q_ref, *rest, tk, post):
    kv_refs, (g_ref, o_ref, a1_scr, a2_scr) = rest[:-4], rest[-4:]

    @pl.when(pl.program_id(0) == 0)
    def _():
        a1_scr[...] = jnp.ones(a1_scr.shape, F32)
        a2_scr[...] = jnp.ones(a2_scr.shape, F32)

    def finish(a_scr):
        a = a_scr[...]
        return a[:, :HEAD_W] * (1.0 / a[:, HEAD_W:HEAD_W + 1])

    o = finish(a1_scr) - lam_ref[0] * finish(a2_scr)
    ms = jnp.sum(o * o, axis=-1, keepdims=True) * (1.0 / HEAD_W)
    o_ref[...] = (o * lax.rsqrt(ms + EPS) * g_ref[...] * post).astype(BF16)

    q = q_ref[...]

    def component(c):
        m = a = None
        for src in range(0, len(kv_refs), 3):
            kt_ref, v_ref = kv_refs[src + c], kv_refs[src + 2]
            for c0 in range(0, kt_ref.shape[1], tk):
                s = jnp.dot(q, kt_ref[:, c0:c0 + tk], preferred_element_type=F32)
                mc = jnp.max(s, axis=-1, keepdims=True)
                mn = mc if m is None else jnp.maximum(m, mc)
                p = jnp.exp2(s - mn).astype(BF16)
                pv = jnp.dot(p, v_ref[c0:c0 + tk, :], preferred_element_type=F32)
                a = pv if m is None else jnp.exp2(m - mn) * a + pv
                m = mn
        return a

    a1_scr[...] = component(0)
    a2_scr[...] = component(1)


def _attention(lam, q, kv_sets, g, *, nb, tq_total, post):
    tq = _pick(tq_total, (512, 256, 128))
    nq = tq_total // tq
    tk = _pick(math.gcd(*[s[0].shape[2] for s in kv_sets]), (256, 128))
    n = nb * DIFF_HEADS * nq

    def tile(t):
        return t // (DIFF_HEADS * nq), (t // nq) % DIFF_HEADS, t % nq

    def at(f, lag):
        return lambda s: f(*tile(jnp.maximum(s - 1, 0) if lag else jnp.minimum(s, n - 1)))

    qmap = lambda b, h, i: (b * nq + i, h)
    kv_specs, kv_args = [], []
    for k1t, k2t, vx in kv_sets:
        t_src = k1t.shape[2]
        kt_spec = pl.BlockSpec((None, HEAD_W, t_src), at(lambda b, h, i: (b, h, 0), False))
        kv_specs += [kt_spec, kt_spec, pl.BlockSpec((None, t_src, 2 * HEAD_W), at(lambda b, h, i: (b, 0, h), False))]
        kv_args += [k1t, k2t, vx]
    return pl.pallas_call(
        functools.partial(_attn_kernel, tk=tk, post=post),
        grid=(n + 1,),
        in_specs=[pl.BlockSpec(memory_space=pltpu.SMEM), pl.BlockSpec((tq, HEAD_W), at(qmap, False))]
        + kv_specs + [pl.BlockSpec((1, HEAD_W), lambda s: (0, 0))],
        out_specs=pl.BlockSpec((tq, HEAD_W), at(qmap, True)),
        out_shape=jax.ShapeDtypeStruct(q.shape, BF16),
        scratch_shapes=[pltpu.VMEM((tq, 2 * HEAD_W), F32), pltpu.VMEM((tq, 2 * HEAD_W), F32)],
        compiler_params=_cparams("arbitrary"),
        name="attention",
    )(lam, q, *kv_args, g)


def _mixed_residual(ret_ref, dif_ref, w_ref, x_ref, gate):
    half = w_ref.shape[0] // 2
    mix = (jnp.dot(ret_ref[...], w_ref[:half, :], preferred_element_type=F32)
           + jnp.dot(dif_ref[...], w_ref[half:, :], preferred_element_type=F32))
    return x_ref[...] + gate * mix


def _outproj_kernel(ret_ref, dif_ref, w_ref, x_ref, mod_ref, xo_ref, *, tpb, d):
    (gate,) = _mod_rows(mod_ref, pl.program_id(0) // tpb, d, (2,))
    xo_ref[...] = _mixed_residual(ret_ref, dif_ref, w_ref, x_ref, gate)


def _outproj(ret, dif, w_bf, x2, modl, *, tpb, tm):
    r, d = x2.shape
    row = lambda i: (i, 0)
    full = lambda i: (0, 0)
    return pl.pallas_call(
        functools.partial(_outproj_kernel, tpb=tpb, d=d),
        grid=(r // tm,),
        in_specs=[pl.BlockSpec((tm, ret.shape[1]), row),
                  pl.BlockSpec((tm, dif.shape[1]), row),
                  pl.BlockSpec(w_bf.shape, full),
                  pl.BlockSpec((tm, d), row),
                  pl.BlockSpec(modl.shape, full)],
        out_specs=pl.BlockSpec((tm, d), row),
        out_shape=jax.ShapeDtypeStruct((r, d), F32),
        compiler_params=_cparams("parallel"),
        name="outproj",
    )(ret, dif, w_bf, x2, modl)


def _swiglu_tile(h, wg_ref, wu_ref, wd_ref, fc):
    ff = wg_ref.shape[-1]
    acc = None
    for c0 in range(0, ff, fc):
        g = jnp.dot(h, wg_ref[0, :, c0:c0 + fc].astype(BF16), preferred_element_type=F32)
        u = jnp.dot(h, wu_ref[0, :, c0:c0 + fc].astype(BF16), preferred_element_type=F32)
        a = (_silu(g) * u).astype(BF16)
        part = jnp.dot(a, wd_ref[0, c0:c0 + fc, :].astype(BF16), preferred_element_type=F32)
        acc = part if acc is None else acc + part
    return acc


def _mix_ffn_kernel(ret_ref, dif_ref, wo_ref, x_ref, mod_ref, g_ref, wg_ref, wu_ref, wd_ref, o_ref,
                    *, row0, tpb, d, fc):
    r = row0 + pl.program_id(0) // tpb
    gate1, shift, scale, gate2 = _mod_rows(mod_ref, r, d, (2, 3, 4, 5))
    x = _mixed_residual(ret_ref, dif_ref, wo_ref, x_ref, gate1)
    h = _norm_mod(x, g_ref[...], shift, scale).astype(BF16)
    o_ref[...] = x + gate2 * _swiglu_tile(h, wg_ref, wu_ref, wd_ref, fc)


def _ff_chunk(ff):
    return _pick(ff, (256, 128))


def _mix_ffn(ret, dif, wo_bf, x2, modl, g2, wg, wu, wd, *, row0, tpb, tm):
    r, d = x2.shape
    ff = wg.shape[-1]
    row = lambda i: (i, 0)
    full = lambda i: (0, 0)
    wfull = lambda i: (0, 0, 0)
    return pl.pallas_call(
        functools.partial(_mix_ffn_kernel, row0=row0, tpb=tpb, d=d, fc=_ff_chunk(ff)),
        grid=(r // tm,),
        in_specs=[pl.BlockSpec((tm, ret.shape[1]), row),
                  pl.BlockSpec((tm, dif.shape[1]), row),
                  pl.BlockSpec(wo_bf.shape, full),
                  pl.BlockSpec((tm, d), row),
                  pl.BlockSpec(modl.shape, full),
                  pl.BlockSpec((1, d), full),
                  pl.BlockSpec((1, d, ff), wfull, pipeline_mode=pl.Buffered(1)),
                  pl.BlockSpec((1, d, ff), wfull, pipeline_mode=pl.Buffered(1)),
                  pl.BlockSpec((1, ff, d), wfull, pipeline_mode=pl.Buffered(1))],
        out_specs=pl.BlockSpec((tm, d), row),
        out_shape=jax.ShapeDtypeStruct((r, d), F32),
        compiler_params=_cparams("parallel"),
        name="mix_ffn",
    )(ret, dif, wo_bf, x2, modl, g2, wg, wu, wd)


def _rows_to_wide(ref, tm, lead=()):
    return jnp.concatenate([ref[lead + (pl.ds(k, tm, stride=SUBLANES), slice(None))] for k in range(SUBLANES)],
                           axis=1)


def _wide_to_rows(ref, val):
    tm = val.shape[0]
    for k in range(SUBLANES):
        ref[pl.ds(k, tm, stride=SUBLANES), :] = val[:, k * LANES:(k + 1) * LANES]


def _row_tile(ref, r):
    return ref.at[pl.ds(pl.multiple_of(r * SUBLANES, SUBLANES), SUBLANES), :]


ROW_DMA_UNROLL = 16


def _start_rows(n, copy):
    def body(i, carry):
        for u in range(ROW_DMA_UNROLL):
            copy(i * ROW_DMA_UNROLL + u).start(priority=u % 2)
        return carry
    lax.fori_loop(0, n // ROW_DMA_UNROLL, body, 0)


def _wait_rows(n, copy):
    def body(i, carry):
        for u in range(ROW_DMA_UNROLL):
            copy(i * ROW_DMA_UNROLL + u).wait()
        return carry
    lax.fori_loop(0, n // ROW_DMA_UNROLL, body, 0)


def _ffn_group_kernel(te_ref, nu_ref, h_ref, wg_ref, wu_ref, wd_ref, o_ref, *, tm, fc):
    i = pl.program_id(0)

    @pl.when(i < nu_ref[0])
    def _():
        _wide_to_rows(o_ref, _swiglu_tile(_rows_to_wide(h_ref, tm).astype(BF16), wg_ref, wu_ref, wd_ref, fc))

    @pl.when(i >= nu_ref[0])
    def _():
        o_ref[...] = jnp.zeros(o_ref.shape, F32)


def _ffn_grouped(tile_expert, n_used, hs, wg, wu, wd, *, tm):
    d, ff = wg.shape[-2:]
    assert d == SUBLANES * LANES
    rows = hs.shape[0] // SUBLANES
    row = lambda i, te, nu: (i, 0)
    wsel = lambda i, te, nu: (te[i], 0, 0)
    return pl.pallas_call(
        functools.partial(_ffn_group_kernel, tm=tm, fc=_ff_chunk(ff)),
        grid_spec=pltpu.PrefetchScalarGridSpec(
            num_scalar_prefetch=2,
            grid=(rows // tm,),
            in_specs=[pl.BlockSpec((tm * SUBLANES, LANES), row),
                      pl.BlockSpec((1, d, ff), wsel),
                      pl.BlockSpec((1, d, ff), wsel),
                      pl.BlockSpec((1, ff, d), wsel)],
            out_specs=pl.BlockSpec((tm * SUBLANES, LANES), row)),
        out_shape=jax.ShapeDtypeStruct(hs.shape, F32),
        compiler_params=_cparams("arbitrary"),
        name="ffn_grouped",
    )(tile_expert, n_used, hs, wg, wu, wd)


def _route_kernel(x_ref, mod_ref, g_ref, rw_ref, rb_ref, ri_ref, rwt_ref, cnt_ref, carry, *, tpb, d, ne):
    i = pl.program_id(0)

    @pl.when(i == 0)
    def _():
        carry[...] = jnp.zeros(carry.shape, F32)

    r = i // tpb
    shift, scale = _mod_rows(mod_ref, r, d, (3, 4))
    h = _norm_mod(x_ref[...], g_ref[...], shift, scale)
    tm = h.shape[0]
    nt = (((1,), (1,)), ((), ()))
    logits = lax.dot_general(rw_ref[...], h, nt, preferred_element_type=F32, precision=HIGHEST)
    logits = logits + rb_ref[...]
    eid = lax.broadcasted_iota(jnp.int32, (ne, tm), 0)
    m1 = jnp.max(logits, axis=0, keepdims=True)
    i1 = jnp.min(jnp.where(logits == m1, eid, ne), axis=0, keepdims=True)
    rest = jnp.where(eid == i1, -jnp.inf, logits)
    m2 = jnp.max(rest, axis=0, keepdims=True)
    i2 = jnp.min(jnp.where(rest == m2, eid, ne), axis=0, keepdims=True)
    e2 = jnp.exp(m2 - m1)
    w1 = 1.0 / (1.0 + e2)
    w2 = e2 * w1
    sel1 = eid == i1
    sel2 = eid == i2
    member = jnp.where(sel1, 1.0, jnp.where(sel2, 1.0, 0.0))
    tr = lax.broadcasted_iota(jnp.int32, (tm, tm), 0)
    tc = lax.broadcasted_iota(jnp.int32, (tm, tm), 1)
    before = jnp.where(tr < tc, 1.0, 0.0).astype(BF16)
    prefix = jnp.dot(member.astype(BF16), before, preferred_element_type=F32) + carry[:, 0:1]
    rank1 = jnp.sum(jnp.where(sel1, prefix, 0.0), axis=0, keepdims=True)
    rank2 = jnp.sum(jnp.where(sel2, prefix, 0.0), axis=0, keepdims=True)
    zi = jnp.zeros((SUBLANES - 4, tm), jnp.int32)
    ri_ref[...] = jnp.concatenate([i1, i2, rank1.astype(jnp.int32), rank2.astype(jnp.int32), zi], axis=0)
    rwt_ref[...] = jnp.concatenate([w1, w2, jnp.zeros((SUBLANES - 2, tm), F32)], axis=0)
    carry[...] = carry[...] + jnp.sum(member, axis=1, keepdims=True)
    cnt_ref[...] = carry[...]


def _route(x2, modl, g2, rw_t, rb, *, tpb, tm):
    n, d = x2.shape
    ne = rw_t.shape[0]
    return pl.pallas_call(
        functools.partial(_route_kernel, tpb=tpb, d=d, ne=ne),
        grid=(n // tm,),
        in_specs=[pl.BlockSpec((tm, d), lambda i: (i, 0)),
                  pl.BlockSpec(modl.shape, lambda i: (0, 0)),
                  pl.BlockSpec((1, d), lambda i: (0, 0)),
                  pl.BlockSpec((ne, d), lambda i: (0, 0)),
                  pl.BlockSpec((ne, 1), lambda i: (0, 0))],
        out_specs=[pl.BlockSpec((SUBLANES, tm), lambda i: (0, i)),
                   pl.BlockSpec((SUBLANES, tm), lambda i: (0, i)),
                   pl.BlockSpec((ne, LANES), lambda i: (0, 0))],
        out_shape=[jax.ShapeDtypeStruct((SUBLANES, n), jnp.int32),
                   jax.ShapeDtypeStruct((SUBLANES, n), F32),
                   jax.ShapeDtypeStruct((ne, LANES), F32)],
        scratch_shapes=[pltpu.VMEM((ne, LANES), F32)],
        compiler_params=_cparams("arbitrary"),
        name="route",
    )(x2, modl, g2, rw_t, rb)


def _pos_kernel(off_ref, ri_ref, pos_ref, *, ne):
    ri = ri_ref[...]
    e = ri[0:2, :]
    pos = ri[2:4, :]
    for k in range(ne):
        pos = pos + jnp.where(e == k, off_ref[k], 0)
    pos_ref[...] = pos


def _positions(offsets, route_i, *, ne):
    n = route_i.shape[1]
    return pl.pallas_call(
        functools.partial(_pos_kernel, ne=ne),
        in_specs=[pl.BlockSpec(memory_space=pltpu.SMEM),
                  pl.BlockSpec(route_i.shape, lambda: (0, 0))],
        out_specs=pl.BlockSpec((TOP_K, n), lambda: (0, 0)),
        out_shape=jax.ShapeDtypeStruct((TOP_K, n), jnp.int32),
        name="positions",
    )(offsets, route_i)


def _dispatch_kernel(pos_ref, pad_ref, x_ref, mod_ref, g_ref, xs_ref, h_scr, z_scr, sem, zsem, *, tpb, d, n):
    i = pl.program_id(0)
    tm = x_ref.shape[0]
    n_pad = pad_ref.shape[0]
    shift, scale = _mod_rows(mod_ref, i // tpb, d, (3, 4))
    _wide_to_rows(h_scr, _norm_mod(x_ref[...], g_ref[...], shift, scale))

    def copy(j, t):
        dst = pos_ref[j * n + i * tm + t]
        return pltpu.make_async_copy(_row_tile(h_scr, t), _row_tile(xs_ref, dst), sem)

    def zero(k):
        return pltpu.make_async_copy(z_scr, _row_tile(xs_ref, pad_ref[k]), zsem)

    @pl.when(i == 0)
    def _():
        z_scr[...] = jnp.zeros(z_scr.shape, F32)
        _start_rows(n_pad, zero)

    for j in range(TOP_K):
        _start_rows(tm, functools.partial(copy, j))

    @pl.when(i == 0)
    def _():
        _wait_rows(n_pad, zero)

    for j in range(TOP_K):
        _wait_rows(tm, functools.partial(copy, j))


def _dispatch(pos_flat, pad_rows, x2, modl, g2, *, rows, tpb, tm):
    n, d = x2.shape
    assert rows == TOP_K * n + pad_rows.shape[0] and pad_rows.shape[0] % ROW_DMA_UNROLL == 0
    return pl.pallas_call(
        functools.partial(_dispatch_kernel, tpb=tpb, d=d, n=n),
        grid_spec=pltpu.PrefetchScalarGridSpec(
            num_scalar_prefetch=2,
            grid=(n // tm,),
            in_specs=[pl.BlockSpec((tm, d), lambda i, pos, pad: (i, 0)),
                      pl.BlockSpec(modl.shape, lambda i, pos, pad: (0, 0)),
                      pl.BlockSpec((1, d), lambda i, pos, pad: (0, 0))],
            out_specs=pl.BlockSpec(memory_space=pl.ANY),
            scratch_shapes=[pltpu.VMEM((tm * SUBLANES, LANES), F32), pltpu.VMEM((SUBLANES, LANES), F32),
                            pltpu.SemaphoreType.DMA(()), pltpu.SemaphoreType.DMA(())]),
        out_shape=jax.ShapeDtypeStruct((rows * SUBLANES, LANES), F32),
        compiler_params=pltpu.CompilerParams(dimension_semantics=("arbitrary",), vmem_limit_bytes=VMEM_LIMIT,
                                             has_side_effects=True),
        name="dispatch",
    )(pos_flat, pad_rows, x2, modl, g2)


def _combine_kernel(pos_ref, ys_ref, w_ref, x_ref, mod_ref, o_ref, ybuf, sem, *, tpb, d, n, n_steps):
    i = pl.program_id(0)
    tm = x_ref.shape[0]

    def copy(step, slot, j, t):
        src = pos_ref[j * n + step * tm + t]
        return pltpu.make_async_copy(_row_tile(ys_ref, src), _row_tile(ybuf.at[slot, j], t), sem.at[slot])

    def fetch(step, slot):
        for j in range(TOP_K):
            _start_rows(tm, functools.partial(copy, step, slot, j))

    @pl.when(i == 0)
    def _():
        fetch(0, 0)

    @pl.when(i + 1 < n_steps)
    def _():
        fetch(i + 1, (i + 1) % 2)

    slot = i % 2
    for j in range(TOP_K):
        _wait_rows(tm, functools.partial(copy, i, slot, j))

    (gate,) = _mod_rows(mod_ref, i // tpb, d, (5,))
    w = w_ref[...]
    y = w[:, 0:1] * _rows_to_wide(ybuf, tm, (slot, 0)) + w[:, 1:2] * _rows_to_wide(ybuf, tm, (slot, 1))
    o_ref[...] = x_ref[...] + gate * y


def _combine(pos_flat, ys, w_t, x2, modl, *, tpb, tm):
    n, d = x2.shape
    n_steps = n // tm
    row = lambda i, pos: (i, 0)
    return pl.pallas_call(
        functools.partial(_combine_kernel, tpb=tpb, d=d, n=n, n_steps=n_steps),
        grid_spec=pltpu.PrefetchScalarGridSpec(
            num_scalar_prefetch=1,
            grid=(n_steps,),
            in_specs=[pl.BlockSpec(memory_space=pl.ANY),
                      pl.BlockSpec((tm, TOP_K), row),
                      pl.BlockSpec((tm, d), row),
                      pl.BlockSpec(modl.shape, lambda i, pos: (0, 0))],
            out_specs=pl.BlockSpec((tm, d), row),
            scratch_shapes=[pltpu.VMEM((2, TOP_K, tm * SUBLANES, LANES), F32), pltpu.SemaphoreType.DMA((2,))]),
        out_shape=jax.ShapeDtypeStruct((n, d), F32),
        compiler_params=pltpu.CompilerParams(dimension_semantics=("arbitrary",), vmem_limit_bytes=VMEM_LIMIT),
        name="combine",
    )(pos_flat, ys, w_t, x2, modl)


def _rope_tables(t):
    n_freq = QK_HEAD // 4
    rows = t // GRID_W
    row_pos = jnp.repeat(jnp.arange(rows, dtype=F32), GRID_W)
    col_pos = jnp.tile(jnp.arange(GRID_W, dtype=F32), rows)
    inv_freq = ROPE_BASE ** (-jnp.arange(n_freq, dtype=F32) / n_freq)
    ang = jnp.concatenate([row_pos[:, None] * inv_freq, col_pos[:, None] * inv_freq], axis=-1)
    ang = jnp.concatenate([ang, ang], axis=-1)
    sign = jnp.where(jnp.arange(QK_HEAD) < QK_HEAD // 2, -1.0, 1.0).astype(F32)
    rep = LANES // QK_HEAD
    return jnp.tile(jnp.cos(ang), (1, rep)), jnp.tile(jnp.sin(ang) * sign, (1, rep))


def _moe_plan(counts, n_routed, n_tiles, tm):
    tiles = (counts + tm - 1) // tm
    ends = jnp.cumsum(tiles)
    offsets = (ends - tiles) * tm
    n_used = ends[-1]
    ids = jnp.arange(n_tiles, dtype=jnp.int32)
    te = jnp.sum(jnp.minimum(ids, n_used - 1)[:, None] >= ends[None, :], axis=1).astype(jnp.int32)
    seg_end = jnp.concatenate([offsets[1:], jnp.full((1,), n_tiles * tm, offsets.dtype)])
    free = seg_end - (offsets + counts)
    free_end = jnp.cumsum(free)
    k = jnp.arange(n_tiles * tm - n_routed, dtype=jnp.int32)
    in_run = (k[:, None] >= (free_end - free)[None, :]) & (k[:, None] < free_end[None, :])
    pad_rows = jnp.sum(jnp.where(in_run, (offsets + counts - (free_end - free))[None, :], 0), axis=1) + k
    return offsets.astype(jnp.int32), te, n_used.reshape(1).astype(jnp.int32), pad_rows.astype(jnp.int32)


def kernel(x, c, ctx, c_ctx, ada_w, ada_b, norm1_g, norm2_g, w_in, ret_decay_fwd, ret_decay_bwd, ret_gn_g, diff_qn_g, diff_kn_g, lam_q1, lam_k1, lam_q2, lam_k2, diff_subln_g, w_out, ffn_w_gate, ffn_w_up, ffn_w_down, router_w, router_b, moe_w_gate, moe_w_up, moe_w_down):
    b, t, d = x.shape
    tc = ctx.shape[1]
    depth = ada_w.shape[0]
    ne = router_w.shape[-1]
    w = 4 * HEAD_W
    assert b + 1 <= SUBLANES and t % GRID_W == 0 and t % CHUNK == 0 and tc % CHUNK == 0
    assert w_in.shape[-1] == 7 * w and d % LANES == 0

    c8 = jnp.zeros((SUBLANES, d), F32).at[:b].set(c).at[b].set(c_ctx)
    mods = _ada(c8, ada_w, ada_b)

    cos, sin = _rope_tables(t)
    cos_c = jnp.ones((b * tc, LANES), F32)
    sin_c = jnp.zeros((b * tc, LANES), F32)
    tm = _pick(t, (512, 256, 128))
    tmc = _pick(tc, (256, 128))
    tpb = t // tm
    ntc = (b * tc) // tmc
    assert t % tmc == 0
    rep4 = lambda g: jnp.tile(g.astype(F32), w // g.shape[0]).reshape(1, w)

    xl = x.reshape(b * t, d)
    xc = ctx.reshape(b * tc, d)
    zero_state = jnp.zeros((b, RET_HEADS, HEAD_W, HEAD_W), F32)

    for l in range(depth):
        ctx_out = l < depth - 1
        lam_init = 0.8 - 0.6 * math.exp(-0.3 * l)
        modl = mods[l]
        g1 = norm1_g[l].reshape(1, d)
        g2 = norm2_g[l].reshape(1, d)
        w_bf = w_in[l].astype(BF16)
        wo_bf = w_out[l].astype(BF16)
        qg, kg = rep4(diff_qn_g[l]), rep4(diff_kn_g[l])
        gn = ret_gn_g[l].reshape(1, w)
        sg = diff_subln_g[l].reshape(1, HEAD_W)
        lg = jnp.stack([jax.nn.log_sigmoid(ret_decay_fwd[l].astype(F32)),
                        jax.nn.log_sigmoid(ret_decay_bwd[l].astype(F32))])
        lam = (jnp.exp(jnp.sum(lam_q1[l].astype(F32) * lam_k1[l].astype(F32)))
               - jnp.exp(jnp.sum(lam_q2[l].astype(F32) * lam_k2[l].astype(F32))) + lam_init).reshape(1)
        post = 1.0 - lam_init

        rq, rk, rv, rg, dq, *kv_l = _inproj(
            xl, modl, g1, w_bf, qg, kg, cos, sin, row0=0, mtpb=tpb, tpb=tpb, tm=tm)
        rqc, rkc, rvc, rgc, dqc, *kv_c = _inproj(
            xc, modl, g1, w_bf, qg, kg, cos_c, sin_c, row0=b, mtpb=ntc, tpb=tc // tmc, tm=tmc)

        ret_c, sc_f, sc_b = _retention(lg, rqc, rkc, rvc, rgc, gn, zero_state, zero_state, nb=b, t=tc)
        ret, _, _ = _retention(lg, rq, rk, rv, rg, gn, sc_f, sc_b, nb=b, t=t)

        dif = _attention(lam, dq, (kv_l, kv_c), sg, nb=b, tq_total=t, post=post)
        if ctx_out:
            dif_c = _attention(lam, dqc, (kv_c,), sg, nb=b, tq_total=tc, post=post)

        if l % 2 == 0:
            j = l // 2
            wg, wu, wd = ffn_w_gate[j:j + 1], ffn_w_up[j:j + 1], ffn_w_down[j:j + 1]
            xl = _mix_ffn(ret, dif, wo_bf, xl, modl, g2, wg, wu, wd, row0=0, tpb=tpb, tm=tm)
            if ctx_out:
                xc = _mix_ffn(ret_c, dif_c, wo_bf, xc, modl, g2, wg, wu, wd, row0=b, tpb=ntc, tm=tmc)
        else:
            if ctx_out:
                raise NotImplementedError("routed channel mixer on context tokens")
            j = l // 2
            n = b * t
            tme = _pick(t, (512, 256, 128))
            n_tiles = (n * TOP_K) // tme + ne
            xl = _outproj(ret, dif, wo_bf, xl, modl, tpb=tpb, tm=tm)
            route_i, route_w, counts = _route(xl, modl, g2, router_w[j].T.astype(F32),
                                              router_b[j].reshape(ne, 1).astype(F32), tpb=tpb, tm=tm)
            offsets, tile_expert, n_used, pad_rows = _moe_plan(
                counts[:, 0].astype(jnp.int32), TOP_K * n, n_tiles, tme)
            pos = _positions(offsets, route_i, ne=ne)
            pos_flat = pos.reshape(TOP_K * n)
            xs = _dispatch(pos_flat, pad_rows, xl, modl, g2, rows=n_tiles * tme, tpb=t // tme, tm=tme)
            ys = _ffn_grouped(tile_expert, n_used, xs,
                              moe_w_gate[j].astype(BF16), moe_w_up[j].astype(BF16),
                              moe_w_down[j].astype(BF16), tm=tme)
            xl = _combine(pos_flat, ys, route_w[:TOP_K].T, xl, modl, tpb=t // tme, tm=tme)
    return xl.reshape(b, t, d)
```

```python
import functools
import math

import jax
import jax.numpy as jnp
from jax import lax
from jax.experimental import pallas as pl
from jax.experimental.pallas import tpu as pltpu

F32 = jnp.float32
BF16 = jnp.bfloat16
HIGHEST = lax.Precision.HIGHEST

EPS = 1e-6
LOG2E = math.log2(math.e)
ROPE_BASE = 10000.0
GRID_W = 64
RET_HEADS = 4
DIFF_HEADS = 4
HEAD_W = 128
QK_HEAD = 64
CHUNK = 128
TOP_K = 2
LANES = 128
SUBLANES = 8
VMEM_LIMIT = 56 * 1024 * 1024


def _cparams(*sem):
    return pltpu.CompilerParams(dimension_semantics=sem, vmem_limit_bytes=VMEM_LIMIT)


def _silu(x):
    return x * (1.0 / (1.0 + jnp.exp(-x)))


def _pick(n, cands):
    for c in cands:
        if n % c == 0:
            return c
    raise ValueError(f"no tile for {n}")


def _ada_kernel(c_ref, w_ref, b_ref, o_ref):
    s = _silu(c_ref[...])
    o_ref[0] = jnp.dot(s, w_ref[0], preferred_element_type=F32, precision=HIGHEST) + b_ref[0]


def _ada(c8, ada_w, ada_b):
    depth, d, n = ada_w.shape
    tn = _pick(n, (1536, 1024, 512, 256, 128))
    return pl.pallas_call(
        _ada_kernel,
        grid=(depth, n // tn),
        in_specs=[pl.BlockSpec((SUBLANES, d), lambda l, j: (0, 0)),
                  pl.BlockSpec((1, d, tn), lambda l, j: (l, 0, j)),
                  pl.BlockSpec((1, 1, tn), lambda l, j: (l, 0, j))],
        out_specs=pl.BlockSpec((1, SUBLANES, tn), lambda l, j: (l, 0, j)),
        out_shape=jax.ShapeDtypeStruct((depth, SUBLANES, n), F32),
        compiler_params=_cparams("parallel", "parallel"),
        name="ada",
    )(c8, ada_w, ada_b.reshape(depth, 1, n))


def _mod_rows(mod_ref, r, d, ks):
    return [mod_ref[pl.ds(r, 1), k * d:(k + 1) * d] for k in ks]


def _norm_mod(x, g, shift, scale):
    ms = jnp.sum(x * x, axis=-1, keepdims=True) * (1.0 / x.shape[-1])
    y = x * lax.rsqrt(ms + EPS) * g
    return y * (1.0 + scale) + shift


def _inproj_kernel(x_ref, mod_ref, g_ref, w_ref, qg_ref, kg_ref, cos_ref, sin_ref,
                   rq_ref, rk_ref, rv_ref, rg_ref, dq_ref, dk1_ref, dk2_ref, dv_ref, *, row0, mtpb, d):
    i = pl.program_id(0)
    r = row0 + i // mtpb
    shift, scale = _mod_rows(mod_ref, r, d, (0, 1))
    h = _norm_mod(x_ref[...], g_ref[...], shift, scale).astype(BF16)
    w = 4 * HEAD_W

    def proj(j):
        return jnp.dot(h, w_ref[:, j * w:(j + 1) * w], preferred_element_type=F32)

    rq_ref[...] = proj(0).astype(BF16)
    rk_ref[...] = (proj(1) * (HEAD_W ** -0.5)).astype(BF16)
    rv_ref[...] = proj(2).astype(BF16)
    rg_ref[...] = proj(3).astype(BF16)

    tm = h.shape[0]
    lane = lax.broadcasted_iota(jnp.int32, (tm, w), 1)
    rowi = lax.broadcasted_iota(jnp.int32, (w, w), 0) // QK_HEAD
    coli = lax.broadcasted_iota(jnp.int32, (w, w), 1) // QK_HEAD
    seg = jnp.where(rowi == coli, 1.0, 0.0).astype(BF16)
    low = (lane % QK_HEAD) < (QK_HEAD // 2)
    cos = jnp.concatenate([cos_ref[...]] * (w // LANES), axis=1)
    sin = jnp.concatenate([sin_ref[...]] * (w // LANES), axis=1)

    def qk_norm_rope(a, gain):
        ss = jnp.dot((a * a).astype(BF16), seg, preferred_element_type=F32)
        y = a * lax.rsqrt(ss * (1.0 / QK_HEAD) + EPS) * gain
        rot = jnp.where(low, pltpu.roll(y, w - QK_HEAD // 2, 1), pltpu.roll(y, QK_HEAD // 2, 1))
        return y * cos + rot * sin

    q = qk_norm_rope(proj(4), qg_ref[...]) * (QK_HEAD ** -0.5 * LOG2E)
    dq_ref[...] = q.astype(BF16)
    kt = qk_norm_rope(proj(5), kg_ref[...]).T
    first = (lax.broadcasted_iota(jnp.int32, (w, tm), 0) % HEAD_W) < QK_HEAD
    k1t = jnp.where(first, kt, 0.0).astype(BF16)
    k2t = jnp.where(first, 0.0, kt).astype(BF16)
    v = proj(6).astype(BF16)
    ones_col = jnp.where(lax.broadcasted_iota(jnp.int32, (tm, HEAD_W), 1) == 0, 1.0, 0.0).astype(BF16)
    pieces = []
    for hd in range(DIFF_HEADS):
        pieces += [v[:, hd * HEAD_W:(hd + 1) * HEAD_W], ones_col]
    dk1_ref[...] = k1t
    dk2_ref[...] = k2t
    dv_ref[...] = jnp.concatenate(pieces, axis=1)


def _inproj(x2, modl, g1, w_bf, qg, kg, cos, sin, *, row0, mtpb, tpb, tm):
    r, d = x2.shape
    w = 4 * HEAD_W
    n_tiles = r // tm
    nb, t_kv = n_tiles // tpb, tpb * tm
    row = lambda i: (i, 0)
    full = lambda i: (0, 0)
    tab = lambda i: (i % tpb, 0)
    out = jax.ShapeDtypeStruct((r, w), BF16)
    kt_spec = pl.BlockSpec((None, w, tm), lambda i: (i // tpb, 0, i % tpb))
    kt_shape = jax.ShapeDtypeStruct((nb, w, t_kv), BF16)
    return pl.pallas_call(
        functools.partial(_inproj_kernel, row0=row0, mtpb=mtpb, d=d),
        grid=(n_tiles,),
        in_specs=[pl.BlockSpec((tm, d), row),
                  pl.BlockSpec(modl.shape, full),
                  pl.BlockSpec((1, d), full),
                  pl.BlockSpec(w_bf.shape, full),
                  pl.BlockSpec((1, w), full),
                  pl.BlockSpec((1, w), full),
                  pl.BlockSpec((tm, LANES), tab),
                  pl.BlockSpec((tm, LANES), tab)],
        out_specs=[pl.BlockSpec((tm, w), row)] * 5
        + [kt_spec, kt_spec, pl.BlockSpec((None, tm, 2 * w), lambda i: (i // tpb, i % tpb, 0))],
        out_shape=[out] * 5 + [kt_shape, kt_shape, jax.ShapeDtypeStruct((nb, t_kv, 2 * w), BF16)],
        compiler_params=_cparams("parallel"),
        name="inproj",
    )(x2, modl, g1, w_bf, qg, kg, cos, sin)


def _ret_kernel(lg_ref, q_ref, k_ref, v_ref, g_ref, gn_ref, s0f_ref, s0b_ref,
                o_ref, sf_ref, sb_ref, of_scr, ob_scr, *, c, n_chunks):
    hd = pl.program_id(1)
    lgf = lg_ref[0, hd]
    lgb = lg_ref[1, hd]
    rel = (lax.broadcasted_iota(jnp.int32, (c, c), 0) - lax.broadcasted_iota(jnp.int32, (c, c), 1)).astype(F32)
    dmat_f = jnp.where(rel >= 0, jnp.exp(lgf * jnp.maximum(rel, 0.0)), 0.0)
    dmat_b = jnp.where(rel <= 0, jnp.exp(lgb * jnp.maximum(-rel, 0.0)), 0.0)
    row = lax.broadcasted_iota(jnp.int32, (c, HEAD_W), 0).astype(F32)
    zeta_f = jnp.exp(lgf * (c - 1.0 - row))
    xi_f = jnp.exp(lgf * (row + 1.0))
    zeta_b = jnp.exp(lgb * row)
    xi_b = jnp.exp(lgb * (c - row))
    dec_f = jnp.exp(jnp.full((HEAD_W, HEAD_W), lgf * c, F32))
    dec_b = jnp.exp(jnp.full((HEAD_W, HEAD_W), lgb * c, F32))

    sf_ref[0, 0] = s0f_ref[0, 0]
    sb_ref[0, 0] = s0b_ref[0, 0]

    nt = (((1,), (1,)), ((), ()))

    def one(ci, dmat, zeta, xi, dec, s_ref, o_scr):
        rows = pl.ds(pl.multiple_of(ci * c, c), c)
        q = q_ref[rows, :]
        k = k_ref[rows, :]
        v = v_ref[rows, :]
        s = s_ref[0, 0]
        sc = lax.dot_general(q, k, nt, preferred_element_type=F32) * dmat
        o = jnp.dot(sc.astype(BF16), v, preferred_element_type=F32)
        o = o + xi * jnp.dot(q, s.astype(BF16), preferred_element_type=F32)
        o_scr[rows, :] = o
        kzt = (k.astype(F32) * zeta).T.astype(BF16)
        s_ref[0, 0] = dec * s + jnp.dot(kzt, v, preferred_element_type=F32)

    unroll = _pick(n_chunks, (4, 2, 1))

    def body(i, carry):
        for u in range(unroll):
            ci = i * unroll + u
            one(ci, dmat_f, zeta_f, xi_f, dec_f, sf_ref, of_scr)
            one(n_chunks - 1 - ci, dmat_b, zeta_b, xi_b, dec_b, sb_ref, ob_scr)
        return carry

    lax.fori_loop(0, n_chunks // unroll, body, 0)

    gn = gn_ref[...]
    fr = unroll * c

    def fin(ci, carry):
        rows = pl.ds(pl.multiple_of(ci * fr, fr), fr)
        y = of_scr[rows, :] + ob_scr[rows, :]
        mu = jnp.sum(y, axis=-1, keepdims=True) * (1.0 / HEAD_W)
        yc = y - mu
        var = jnp.sum(yc * yc, axis=-1, keepdims=True) * (1.0 / HEAD_W)
        yn = yc * lax.rsqrt(var + EPS) * gn
        o_ref[rows, :] = (_silu(g_ref[rows, :].astype(F32)) * yn).astype(BF16)
        return carry

    lax.fori_loop(0, n_chunks // unroll, fin, 0)


def _retention(lg, rq, rk, rv, rg, gn, s0f, s0b, *, nb, t):
    c = _pick(t, (256, CHUNK))
    n_chunks = t // c
    blk = pl.BlockSpec((t, HEAD_W), lambda b, h: (b, h))
    sblk = pl.BlockSpec((1, 1, HEAD_W, HEAD_W), lambda b, h: (b, h, 0, 0))
    s_shape = jax.ShapeDtypeStruct((nb, RET_HEADS, HEAD_W, HEAD_W), F32)
    return pl.pallas_call(
        functools.partial(_ret_kernel, c=c, n_chunks=n_chunks),
        grid=(nb, RET_HEADS),
        in_specs=[pl.BlockSpec(memory_space=pltpu.SMEM),
                  blk, blk, blk, blk,
                  pl.BlockSpec((1, HEAD_W), lambda b, h: (0, h)),
                  sblk, sblk],
        out_specs=[blk, sblk, sblk],
        out_shape=[jax.ShapeDtypeStruct(rq.shape, BF16), s_shape, s_shape],
        scratch_shapes=[pltpu.VMEM((t, HEAD_W), F32), pltpu.VMEM((t, HEAD_W), F32)],
        compiler_params=_cparams("parallel", "parallel"),
        name="retention",
    )(lg, rq, rk, rv, rg, gn, s0f, s0b)


def _attn_kernel(lam_ref, q_ref, *rest, tk, post):
    kv_refs, (g_ref, o_ref, a1_scr, a2_scr) = rest[:-4], rest[-4:]

    @pl.when(pl.program_id(0) == 0)
    def _():
        a1_scr[...] = jnp.ones(a1_scr.shape, F32)
        a2_scr[...] = jnp.ones(a2_scr.shape, F32)

    def finish(a_scr):
        a = a_scr[...]
        return a[:, :HEAD_W] * (1.0 / a[:, HEAD_W:HEAD_W + 1])

    o = finish(a1_scr) - lam_ref[0] * finish(a2_scr)
    ms = jnp.sum(o * o, axis=-1, keepdims=True) * (1.0 / HEAD_W)
    o_ref[...] = (o * lax.rsqrt(ms + EPS) * g_ref[...] * post).astype(BF16)

    q = q_ref[...]

    def component(c):
        m = a = None
        for src in range(0, len(kv_refs), 3):
            kt_ref, v_ref = kv_refs[src + c], kv_refs[src + 2]
            for c0 in range(0, kt_ref.shape[1], tk):
                s = jnp.dot(q, kt_ref[:, c0:c0 + tk], preferred_element_type=F32)
                mc = jnp.max(s, axis=-1, keepdims=True)
                mn = mc if m is None else jnp.maximum(m, mc)
                p = jnp.exp2(s - mn).astype(BF16)
                pv = jnp.dot(p, v_ref[c0:c0 + tk, :], preferred_element_type=F32)
                a = pv if m is None else jnp.exp2(m - mn) * a + pv
                m = mn
        return a

    a1_scr[...] = component(0)
    a2_scr[...] = component(1)


def _attention(lam, q, kv_sets, g, *, nb, tq_total, post):
    tq = _pick(tq_total, (512, 256, 128))
    nq = tq_total // tq
    tk = _pick(math.gcd(*[s[0].shape[2] for s in kv_sets]), (256, 128))
    n = nb * DIFF_HEADS * nq

    def tile(t):
        return t // (DIFF_HEADS * nq), (t // nq) % DIFF_HEADS, t % nq

    def at(f, lag):
        return lambda s: f(*tile(jnp.maximum(s - 1, 0) if lag else jnp.minimum(s, n - 1)))

    qmap = lambda b, h, i: (b * nq + i, h)
    kv_specs, kv_args = [], []
    for k1t, k2t, vx in kv_sets:
        t_src = k1t.shape[2]
        kt_spec = pl.BlockSpec((None, HEAD_W, t_src), at(lambda b, h, i: (b, h, 0), False))
        kv_specs += [kt_spec, kt_spec, pl.BlockSpec((None, t_src, 2 * HEAD_W), at(lambda b, h, i: (b, 0, h), False))]
        kv_args += [k1t, k2t, vx]
    return pl.pallas_call(
        functools.partial(_attn_kernel, tk=tk, post=post),
        grid=(n + 1,),
        in_specs=[pl.BlockSpec(memory_space=pltpu.SMEM), pl.BlockSpec((tq, HEAD_W), at(qmap, False))]
        + kv_specs + [pl.BlockSpec((1, HEAD_W), lambda s: (0, 0))],
        out_specs=pl.BlockSpec((tq, HEAD_W), at(qmap, True)),
        out_shape=jax.ShapeDtypeStruct(q.shape, BF16),
        scratch_shapes=[pltpu.VMEM((tq, 2 * HEAD_W), F32), pltpu.VMEM((tq, 2 * HEAD_W), F32)],
        compiler_params=_cparams("arbitrary"),
        name="attention",
    )(lam, q, *kv_args, g)


def _mixed_residual(ret_ref, dif_ref, w_ref, x_ref, gate):
    half = w_ref.shape[0] // 2
    mix = (jnp.dot(ret_ref[...], w_ref[:half, :], preferred_element_type=F32)
           + jnp.dot(dif_ref[...], w_ref[half:, :], preferred_element_type=F32))
    return x_ref[...] + gate * mix


def _swiglu_tile(h, wg_ref, wu_ref, wd_ref, fc):
    ff = wg_ref.shape[-1]
    acc = None
    for c0 in range(0, ff, fc):
        g = jnp.dot(h, wg_ref[0, :, c0:c0 + fc].astype(BF16), preferred_element_type=F32)
        u = jnp.dot(h, wu_ref[0, :, c0:c0 + fc].astype(BF16), preferred_element_type=F32)
        a = (_silu(g) * u).astype(BF16)
        part = jnp.dot(a, wd_ref[0, c0:c0 + fc, :].astype(BF16), preferred_element_type=F32)
        acc = part if acc is None else acc + part
    return acc


def _mix_ffn_kernel(ret_ref, dif_ref, wo_ref, x_ref, mod_ref, g_ref, wg_ref, wu_ref, wd_ref, o_ref,
                    *, row0, tpb, d, fc):
    r = row0 + pl.program_id(0) // tpb
    gate1, shift, scale, gate2 = _mod_rows(mod_ref, r, d, (2, 3, 4, 5))
    x = _mixed_residual(ret_ref, dif_ref, wo_ref, x_ref, gate1)
    h = _norm_mod(x, g_ref[...], shift, scale).astype(BF16)
    o_ref[...] = x + gate2 * _swiglu_tile(h, wg_ref, wu_ref, wd_ref, fc)


def _ff_chunk(ff):
    return _pick(ff, (256, 128))


def _mix_ffn(ret, dif, wo_bf, x2, modl, g2, wg, wu, wd, *, row0, tpb, tm):
    r, d = x2.shape
    ff = wg.shape[-1]
    row = lambda i: (i, 0)
    full = lambda i: (0, 0)
    wfull = lambda i: (0, 0, 0)
    return pl.pallas_call(
        functools.partial(_mix_ffn_kernel, row0=row0, tpb=tpb, d=d, fc=_ff_chunk(ff)),
        grid=(r // tm,),
        in_specs=[pl.BlockSpec((tm, ret.shape[1]), row),
                  pl.BlockSpec((tm, dif.shape[1]), row),
                  pl.BlockSpec(wo_bf.shape, full),
                  pl.BlockSpec((tm, d), row),
                  pl.BlockSpec(modl.shape, full),
                  pl.BlockSpec((1, d), full),
                  pl.BlockSpec((1, d, ff), wfull, pipeline_mode=pl.Buffered(1)),
                  pl.BlockSpec((1, d, ff), wfull, pipeline_mode=pl.Buffered(1)),
                  pl.BlockSpec((1, ff, d), wfull, pipeline_mode=pl.Buffered(1))],
        out_specs=pl.BlockSpec((tm, d), row),
        out_shape=jax.ShapeDtypeStruct((r, d), F32),
        compiler_params=_cparams("parallel"),
        name="mix_ffn",
    )(ret, dif, wo_bf, x2, modl, g2, wg, wu, wd)


def _rows_to_wide(ref, tm, lead=()):
    return jnp.concatenate([ref[lead + (pl.ds(k, tm, stride=SUBLANES), slice(None))] for k in range(SUBLANES)],
                           axis=1)


def _wide_to_rows(ref, val):
    tm = val.shape[0]
    for k in range(SUBLANES):
        ref[pl.ds(k, tm, stride=SUBLANES), :] = val[:, k * LANES:(k + 1) * LANES]


def _row_tile(ref, r):
    return ref.at[pl.ds(pl.multiple_of(r * SUBLANES, SUBLANES), SUBLANES), :]


ROW_DMA_UNROLL = 16


def _start_rows(n, copy):
    def body(i, carry):
        for u in range(ROW_DMA_UNROLL):
            copy(i * ROW_DMA_UNROLL + u).start(priority=u % 2)
        return carry
    lax.fori_loop(0, n // ROW_DMA_UNROLL, body, 0)


def _wait_rows(n, copy):
    def body(i, carry):
        for u in range(ROW_DMA_UNROLL):
            copy(i * ROW_DMA_UNROLL + u).wait()
        return carry
    lax.fori_loop(0, n // ROW_DMA_UNROLL, body, 0)


def _ffn_group_kernel(te_ref, nu_ref, h_ref, wg_ref, wu_ref, wd_ref, o_ref, *, tm, fc):
    i = pl.program_id(0)

    @pl.when(i < nu_ref[0])
    def _():
        _wide_to_rows(o_ref, _swiglu_tile(_rows_to_wide(h_ref, tm).astype(BF16), wg_ref, wu_ref, wd_ref, fc))

    @pl.when(i >= nu_ref[0])
    def _():
        o_ref[...] = jnp.zeros(o_ref.shape, F32)


def _ffn_grouped(tile_expert, n_used, hs, wg, wu, wd, *, tm):
    d, ff = wg.shape[-2:]
    assert d == SUBLANES * LANES
    rows = hs.shape[0] // SUBLANES
    row = lambda i, te, nu: (i, 0)
    wsel = lambda i, te, nu: (te[i], 0, 0)
    return pl.pallas_call(
        functools.partial(_ffn_group_kernel, tm=tm, fc=_ff_chunk(ff)),
        grid_spec=pltpu.PrefetchScalarGridSpec(
            num_scalar_prefetch=2,
            grid=(rows // tm,),
            in_specs=[pl.BlockSpec((tm * SUBLANES, LANES), row),
                      pl.BlockSpec((1, d, ff), wsel),
                      pl.BlockSpec((1, d, ff), wsel),
                      pl.BlockSpec((1, ff, d), wsel)],
            out_specs=pl.BlockSpec((tm * SUBLANES, LANES), row)),
        out_shape=jax.ShapeDtypeStruct(hs.shape, F32),
        compiler_params=_cparams("arbitrary"),
        name="ffn_grouped",
    )(tile_expert, n_used, hs, wg, wu, wd)


def _mix_route_kernel(ret_ref, dif_ref, wo_ref, x_ref, mod_ref, g_ref, rw_ref, rb_ref,
                      xo_ref, ri_ref, rwt_ref, cnt_ref, carry, *, tpb, d, ne):
    i = pl.program_id(0)

    @pl.when(i == 0)
    def _():
        carry[...] = jnp.zeros(carry.shape, F32)

    gate, shift, scale = _mod_rows(mod_ref, i // tpb, d, (2, 3, 4))
    x = _mixed_residual(ret_ref, dif_ref, wo_ref, x_ref, gate)
    xo_ref[...] = x
    h = _norm_mod(x, g_ref[...], shift, scale)
    tm = h.shape[0]
    nt = (((1,), (1,)), ((), ()))
    logits = lax.dot_general(rw_ref[...], h, nt, preferred_element_type=F32, precision=HIGHEST)
    logits = logits + rb_ref[...]
    eid = lax.broadcasted_iota(jnp.int32, (ne, tm), 0)
    m1 = jnp.max(logits, axis=0, keepdims=True)
    i1 = jnp.min(jnp.where(logits == m1, eid, ne), axis=0, keepdims=True)
    rest = jnp.where(eid == i1, -jnp.inf, logits)
    m2 = jnp.max(rest, axis=0, keepdims=True)
    i2 = jnp.min(jnp.where(rest == m2, eid, ne), axis=0, keepdims=True)
    e2 = jnp.exp(m2 - m1)
    w1 = 1.0 / (1.0 + e2)
    w2 = e2 * w1
    sel1 = eid == i1
    sel2 = eid == i2
    member = jnp.where(sel1, 1.0, jnp.where(sel2, 1.0, 0.0))
    tr = lax.broadcasted_iota(jnp.int32, (tm, tm), 0)
    tc = lax.broadcasted_iota(jnp.int32, (tm, tm), 1)
    before = jnp.where(tr < tc, 1.0, 0.0).astype(BF16)
    prefix = jnp.dot(member.astype(BF16), before, preferred_element_type=F32) + carry[:, 0:1]
    rank1 = jnp.sum(jnp.where(sel1, prefix, 0.0), axis=0, keepdims=True)
    rank2 = jnp.sum(jnp.where(sel2, prefix, 0.0), axis=0, keepdims=True)
    zi = jnp.zeros((SUBLANES - 4, tm), jnp.int32)
    ri_ref[...] = jnp.concatenate([i1, i2, rank1.astype(jnp.int32), rank2.astype(jnp.int32), zi], axis=0)
    rwt_ref[...] = jnp.concatenate([w1, w2, jnp.zeros((SUBLANES - 2, tm), F32)], axis=0)
    carry[...] = carry[...] + jnp.sum(member, axis=1, keepdims=True)
    cnt_ref[...] = carry[...]


def _mix_route(ret, dif, wo_bf, x2, modl, g2, rw_t, rb, *, tpb, tm):
    n, d = x2.shape
    ne = rw_t.shape[0]
    row = lambda i: (i, 0)
    full = lambda i: (0, 0)
    return pl.pallas_call(
        functools.partial(_mix_route_kernel, tpb=tpb, d=d, ne=ne),
        grid=(n // tm,),
        in_specs=[pl.BlockSpec((tm, ret.shape[1]), row),
                  pl.BlockSpec((tm, dif.shape[1]), row),
                  pl.BlockSpec(wo_bf.shape, full),
                  pl.BlockSpec((tm, d), row),
                  pl.BlockSpec(modl.shape, full),
                  pl.BlockSpec((1, d), full),
                  pl.BlockSpec((ne, d), full),
                  pl.BlockSpec((ne, 1), full)],
        out_specs=[pl.BlockSpec((tm, d), row),
                   pl.BlockSpec((SUBLANES, tm), lambda i: (0, i)),
                   pl.BlockSpec((SUBLANES, tm), lambda i: (0, i)),
                   pl.BlockSpec((ne, LANES), full)],
        out_shape=[jax.ShapeDtypeStruct((n, d), F32),
                   jax.ShapeDtypeStruct((SUBLANES, n), jnp.int32),
                   jax.ShapeDtypeStruct((SUBLANES, n), F32),
                   jax.ShapeDtypeStruct((ne, LANES), F32)],
        scratch_shapes=[pltpu.VMEM((ne, LANES), F32)],
        compiler_params=_cparams("arbitrary"),
        name="mix_route",
    )(ret, dif, wo_bf, x2, modl, g2, rw_t, rb)


def _pos_kernel(off_ref, ri_ref, pos_ref, *, ne):
    ri = ri_ref[...]
    e = ri[0:2, :]
    pos = ri[2:4, :]
    for k in range(ne):
        pos = pos + jnp.where(e == k, off_ref[k], 0)
    pos_ref[...] = pos


def _positions(offsets, route_i, *, ne):
    n = route_i.shape[1]
    return pl.pallas_call(
        functools.partial(_pos_kernel, ne=ne),
        in_specs=[pl.BlockSpec(memory_space=pltpu.SMEM),
                  pl.BlockSpec(route_i.shape, lambda: (0, 0))],
        out_specs=pl.BlockSpec((TOP_K, n), lambda: (0, 0)),
        out_shape=jax.ShapeDtypeStruct((TOP_K, n), jnp.int32),
        name="positions",
    )(offsets, route_i)


def _dispatch_kernel(pos_ref, pad_ref, x_ref, mod_ref, g_ref, xs_ref, h_scr, z_scr, sem, zsem, *, tpb, d, n):
    i = pl.program_id(0)
    tm = x_ref.shape[0]
    n_pad = pad_ref.shape[0]
    shift, scale = _mod_rows(mod_ref, i // tpb, d, (3, 4))
    _wide_to_rows(h_scr, _norm_mod(x_ref[...], g_ref[...], shift, scale))

    def copy(j, t):
        dst = pos_ref[j * n + i * tm + t]
        return pltpu.make_async_copy(_row_tile(h_scr, t), _row_tile(xs_ref, dst), sem)

    def zero(k):
        return pltpu.make_async_copy(z_scr, _row_tile(xs_ref, pad_ref[k]), zsem)

    @pl.when(i == 0)
    def _():
        z_scr[...] = jnp.zeros(z_scr.shape, F32)
        _start_rows(n_pad, zero)

    for j in range(TOP_K):
        _start_rows(tm, functools.partial(copy, j))

    @pl.when(i == 0)
    def _():
        _wait_rows(n_pad, zero)

    for j in range(TOP_K):
        _wait_rows(tm, functools.partial(copy, j))


def _dispatch(pos_flat, pad_rows, x2, modl, g2, *, rows, tpb, tm):
    n, d = x2.shape
    assert rows == TOP_K * n + pad_rows.shape[0] and pad_rows.shape[0] % ROW_DMA_UNROLL == 0
    return pl.pallas_call(
        functools.partial(_dispatch_kernel, tpb=tpb, d=d, n=n),
        grid_spec=pltpu.PrefetchScalarGridSpec(
            num_scalar_prefetch=2,
            grid=(n // tm,),
            in_specs=[pl.BlockSpec((tm, d), lambda i, pos, pad: (i, 0)),
                      pl.BlockSpec(modl.shape, lambda i, pos, pad: (0, 0)),
                      pl.BlockSpec((1, d), lambda i, pos, pad: (0, 0))],
            out_specs=pl.BlockSpec(memory_space=pl.ANY),
            scratch_shapes=[pltpu.VMEM((tm * SUBLANES, LANES), F32), pltpu.VMEM((SUBLANES, LANES), F32),
                            pltpu.SemaphoreType.DMA(()), pltpu.SemaphoreType.DMA(())]),
        out_shape=jax.ShapeDtypeStruct((rows * SUBLANES, LANES), F32),
        compiler_params=pltpu.CompilerParams(dimension_semantics=("arbitrary",), vmem_limit_bytes=VMEM_LIMIT,
                                             has_side_effects=True),
        name="dispatch",
    )(pos_flat, pad_rows, x2, modl, g2)


def _combine_kernel(pos_ref, ys_ref, w_ref, x_ref, mod_ref, o_ref, ybuf, sem, *, tpb, d, n, n_steps):
    i = pl.program_id(0)
    tm = x_ref.shape[0]

    def copy(step, slot, j, t):
        src = pos_ref[j * n + step * tm + t]
        return pltpu.make_async_copy(_row_tile(ys_ref, src), _row_tile(ybuf.at[slot, j], t), sem.at[slot])

    def fetch(step, slot):
        for j in range(TOP_K):
            _start_rows(tm, functools.partial(copy, step, slot, j))

    @pl.when(i == 0)
    def _():
        fetch(0, 0)

    @pl.when(i + 1 < n_steps)
    def _():
        fetch(i + 1, (i + 1) % 2)

    slot = i % 2
    for j in range(TOP_K):
        _wait_rows(tm, functools.partial(copy, i, slot, j))

    (gate,) = _mod_rows(mod_ref, i // tpb, d, (5,))
    w = w_ref[...]
    y = w[:, 0:1] * _rows_to_wide(ybuf, tm, (slot, 0)) + w[:, 1:2] * _rows_to_wide(ybuf, tm, (slot, 1))
    o_ref[...] = x_ref[...] + gate * y


def _combine(pos_flat, ys, w_t, x2, modl, *, tpb, tm):
    n, d = x2.shape
    n_steps = n // tm
    row = lambda i, pos: (i, 0)
    return pl.pallas_call(
        functools.partial(_combine_kernel, tpb=tpb, d=d, n=n, n_steps=n_steps),
        grid_spec=pltpu.PrefetchScalarGridSpec(
            num_scalar_prefetch=1,
            grid=(n_steps,),
            in_specs=[pl.BlockSpec(memory_space=pl.ANY),
                      pl.BlockSpec((tm, TOP_K), row),
                      pl.BlockSpec((tm, d), row),
                      pl.BlockSpec(modl.shape, lambda i, pos: (0, 0))],
            out_specs=pl.BlockSpec((tm, d), row),
            scratch_shapes=[pltpu.VMEM((2, TOP_K, tm * SUBLANES, LANES), F32), pltpu.SemaphoreType.DMA((2,))]),
        out_shape=jax.ShapeDtypeStruct((n, d), F32),
        compiler_params=pltpu.CompilerParams(dimension_semantics=("arbitrary",), vmem_limit_bytes=VMEM_LIMIT),
        name="combine",
    )(pos_flat, ys, w_t, x2, modl)


def _rope_tables(t):
    n_freq = QK_HEAD // 4
    rows = t // GRID_W
    row_pos = jnp.repeat(jnp.arange(rows, dtype=F32), GRID_W)
    col_pos = jnp.tile(jnp.arange(GRID_W, dtype=F32), rows)
    inv_freq = ROPE_BASE ** (-jnp.arange(n_freq, dtype=F32) / n_freq)
    ang = jnp.concatenate([row_pos[:, None] * inv_freq, col_pos[:, None] * inv_freq], axis=-1)
    ang = jnp.concatenate([ang, ang], axis=-1)
    sign = jnp.where(jnp.arange(QK_HEAD) < QK_HEAD // 2, -1.0, 1.0).astype(F32)
    rep = LANES // QK_HEAD
    return jnp.tile(jnp.cos(ang), (1, rep)), jnp.tile(jnp.sin(ang) * sign, (1, rep))


def _moe_plan(counts, n_routed, n_tiles, tm):
    tiles = (counts + tm - 1) // tm
    ends = jnp.cumsum(tiles)
    offsets = (ends - tiles) * tm
    n_used = ends[-1]
    ids = jnp.arange(n_tiles, dtype=jnp.int32)
    te = jnp.sum(jnp.minimum(ids, n_used - 1)[:, None] >= ends[None, :], axis=1).astype(jnp.int32)
    seg_end = jnp.concatenate([offsets[1:], jnp.full((1,), n_tiles * tm, offsets.dtype)])
    free = seg_end - (offsets + counts)
    free_end = jnp.cumsum(free)
    k = jnp.arange(n_tiles * tm - n_routed, dtype=jnp.int32)
    in_run = (k[:, None] >= (free_end - free)[None, :]) & (k[:, None] < free_end[None, :])
    pad_rows = jnp.sum(jnp.where(in_run, (offsets + counts - (free_end - free))[None, :], 0), axis=1) + k
    return offsets.astype(jnp.int32), te, n_used.reshape(1).astype(jnp.int32), pad_rows.astype(jnp.int32)


def kernel(x, c, ctx, c_ctx, ada_w, ada_b, norm1_g, norm2_g, w_in, ret_decay_fwd, ret_decay_bwd, ret_gn_g, diff_qn_g, diff_kn_g, lam_q1, lam_k1, lam_q2, lam_k2, diff_subln_g, w_out, ffn_w_gate, ffn_w_up, ffn_w_down, router_w, router_b, moe_w_gate, moe_w_up, moe_w_down):
    b, t, d = x.shape
    tc = ctx.shape[1]
    depth = ada_w.shape[0]
    ne = router_w.shape[-1]
    w = 4 * HEAD_W
    assert b + 1 <= SUBLANES and t % GRID_W == 0 and t % CHUNK == 0 and tc % CHUNK == 0
    assert w_in.shape[-1] == 7 * w and d % LANES == 0

    c8 = jnp.zeros((SUBLANES, d), F32).at[:b].set(c).at[b].set(c_ctx)
    mods = _ada(c8, ada_w, ada_b)

    cos, sin = _rope_tables(t)
    cos_c = jnp.ones((b * tc, LANES), F32)
    sin_c = jnp.zeros((b * tc, LANES), F32)
    tm = _pick(t, (512, 256, 128))
    tmc = _pick(tc, (256, 128))
    tpb = t // tm
    ntc = (b * tc) // tmc
    assert t % tmc == 0
    rep4 = lambda g: jnp.tile(g.astype(F32), w // g.shape[0]).reshape(1, w)

    xl = x.reshape(b * t, d)
    xc = ctx.reshape(b * tc, d)
    zero_state = jnp.zeros((b, RET_HEADS, HEAD_W, HEAD_W), F32)

    for l in range(depth):
        ctx_out = l < depth - 1
        lam_init = 0.8 - 0.6 * math.exp(-0.3 * l)
        modl = mods[l]
        g1 = norm1_g[l].reshape(1, d)
        g2 = norm2_g[l].reshape(1, d)
        w_bf = w_in[l].astype(BF16)
        wo_bf = w_out[l].astype(BF16)
        qg, kg = rep4(diff_qn_g[l]), rep4(diff_kn_g[l])
        gn = ret_gn_g[l].reshape(1, w)
        sg = diff_subln_g[l].reshape(1, HEAD_W)
        lg = jnp.stack([jax.nn.log_sigmoid(ret_decay_fwd[l].astype(F32)),
                        jax.nn.log_sigmoid(ret_decay_bwd[l].astype(F32))])
        lam = (jnp.exp(jnp.sum(lam_q1[l].astype(F32) * lam_k1[l].astype(F32)))
               - jnp.exp(jnp.sum(lam_q2[l].astype(F32) * lam_k2[l].astype(F32))) + lam_init).reshape(1)
        post = 1.0 - lam_init

        rq, rk, rv, rg, dq, *kv_l = _inproj(
            xl, modl, g1, w_bf, qg, kg, cos, sin, row0=0, mtpb=tpb, tpb=tpb, tm=tm)
        rqc, rkc, rvc, rgc, dqc, *kv_c = _inproj(
            xc, modl, g1, w_bf, qg, kg, cos_c, sin_c, row0=b, mtpb=ntc, tpb=tc // tmc, tm=tmc)

        ret_c, sc_f, sc_b = _retention(lg, rqc, rkc, rvc, rgc, gn, zero_state, zero_state, nb=b, t=tc)
        ret, _, _ = _retention(lg, rq, rk, rv, rg, gn, sc_f, sc_b, nb=b, t=t)

        dif = _attention(lam, dq, (kv_l, kv_c), sg, nb=b, tq_total=t, post=post)
        if ctx_out:
            dif_c = _attention(lam, dqc, (kv_c,), sg, nb=b, tq_total=tc, post=post)

        if l % 2 == 0:
            j = l // 2
            wg, wu, wd = ffn_w_gate[j:j + 1], ffn_w_up[j:j + 1], ffn_w_down[j:j + 1]
            xl = _mix_ffn(ret, dif, wo_bf, xl, modl, g2, wg, wu, wd, row0=0, tpb=tpb, tm=tm)
            if ctx_out:
                xc = _mix_ffn(ret_c, dif_c, wo_bf, xc, modl, g2, wg, wu, wd, row0=b, tpb=ntc, tm=tmc)
        else:
            if ctx_out:
                raise NotImplementedError("routed channel mixer on context tokens")
            j = l // 2
            n = b * t
            tme = _pick(t, (512, 256, 128))
            n_tiles = (n * TOP_K) // tme + ne
            xl, route_i, route_w, counts = _mix_route(
                ret, dif, wo_bf, xl, modl, g2, router_w[j].T.astype(F32),
                router_b[j].reshape(ne, 1).astype(F32), tpb=tpb, tm=tm)
            offsets, tile_expert, n_used, pad_rows = _moe_plan(
                counts[:, 0].astype(jnp.int32), TOP_K * n, n_tiles, tme)
            pos = _positions(offsets, route_i, ne=ne)
            pos_flat = pos.reshape(TOP_K * n)
            xs = _dispatch(pos_flat, pad_rows, xl, modl, g2, rows=n_tiles * tme, tpb=t // tme, tm=tme)
            ys = _ffn_grouped(tile_expert, n_used, xs,
                              moe_w_gate[j].astype(BF16), moe_w_up[j].astype(BF16),
                              moe_w_down[j].astype(BF16), tm=tme)
            xl = _combine(pos_flat, ys, route_w[:TOP_K].T, xl, modl, tpb=t // tme, tm=tme)
    return xl.reshape(b, t, d)
```

```python
import functools
import math

import jax
import jax.numpy as jnp
from jax import lax
from jax.experimental import pallas as pl
from jax.experimental.pallas import tpu as pltpu

F32 = jnp.float32
BF16 = jnp.bfloat16
HIGHEST = lax.Precision.HIGHEST

EPS = 1e-6
LOG2E = math.log2(math.e)
ROPE_BASE = 10000.0
GRID_W = 64
RET_HEADS = 4
DIFF_HEADS = 4
HEAD_W = 128
QK_HEAD = 64
CHUNK = 128
TOP_K = 2
LANES = 128
SUBLANES = 8
VMEM_LIMIT = 56 * 1024 * 1024


def _cparams(*sem):
    return pltpu.CompilerParams(dimension_semantics=sem, vmem_limit_bytes=VMEM_LIMIT)


def _silu(x):
    return x * (1.0 / (1.0 + jnp.exp(-x)))


def _pick(n, cands):
    for c in cands:
        if n % c == 0:
            return c
    raise ValueError(f"no tile for {n}")


def _ada_kernel(c_ref, w_ref, b_ref, o_ref):
    s = _silu(c_ref[...])
    o_ref[0] = jnp.dot(s, w_ref[0], preferred_element_type=F32, precision=HIGHEST) + b_ref[0]


def _ada(c8, ada_w, ada_b):
    depth, d, n = ada_w.shape
    tn = _pick(n, (1536, 1024, 512, 256, 128))
    return pl.pallas_call(
        _ada_kernel,
        grid=(depth, n // tn),
        in_specs=[pl.BlockSpec((SUBLANES, d), lambda l, j: (0, 0)),
                  pl.BlockSpec((1, d, tn), lambda l, j: (l, 0, j)),
                  pl.BlockSpec((1, 1, tn), lambda l, j: (l, 0, j))],
        out_specs=pl.BlockSpec((1, SUBLANES, tn), lambda l, j: (l, 0, j)),
        out_shape=jax.ShapeDtypeStruct((depth, SUBLANES, n), F32),
        compiler_params=_cparams("parallel", "parallel"),
        name="ada",
    )(c8, ada_w, ada_b.reshape(depth, 1, n))


def _mod_rows(mod_ref, r, d, ks):
    return [mod_ref[pl.ds(r, 1), k * d:(k + 1) * d] for k in ks]


def _norm_mod(x, g, shift, scale):
    ms = jnp.sum(x * x, axis=-1, keepdims=True) * (1.0 / x.shape[-1])
    y = x * lax.rsqrt(ms + EPS) * g
    return y * (1.0 + scale) + shift


def _inproj_kernel(x_ref, mod_ref, g_ref, w_ref, qg_ref, kg_ref, cos_ref, sin_ref,
                   rq_ref, rk_ref, rv_ref, rg_ref, dq_ref, dk1_ref, dk2_ref, dv_ref, *, row0, mtpb, d):
    i = pl.program_id(0)
    r = row0 + i // mtpb
    shift, scale = _mod_rows(mod_ref, r, d, (0, 1))
    h = _norm_mod(x_ref[...], g_ref[...], shift, scale).astype(BF16)
    w = 4 * HEAD_W

    def proj(j):
        return jnp.dot(h, w_ref[:, j * w:(j + 1) * w], preferred_element_type=F32)

    rq_ref[...] = proj(0).astype(BF16)
    rk_ref[...] = (proj(1) * (HEAD_W ** -0.5)).astype(BF16)
    rv_ref[...] = proj(2).astype(BF16)
    rg_ref[...] = proj(3).astype(BF16)

    tm = h.shape[0]
    lane = lax.broadcasted_iota(jnp.int32, (tm, w), 1)
    rowi = lax.broadcasted_iota(jnp.int32, (w, w), 0) // QK_HEAD
    coli = lax.broadcasted_iota(jnp.int32, (w, w), 1) // QK_HEAD
    seg = jnp.where(rowi == coli, 1.0, 0.0).astype(BF16)
    low = (lane % QK_HEAD) < (QK_HEAD // 2)
    cos = jnp.concatenate([cos_ref[...]] * (w // LANES), axis=1)
    sin = jnp.concatenate([sin_ref[...]] * (w // LANES), axis=1)

    def qk_norm_rope(a, gain):
        ss = jnp.dot((a * a).astype(BF16), seg, preferred_element_type=F32)
        y = a * lax.rsqrt(ss * (1.0 / QK_HEAD) + EPS) * gain
        rot = jnp.where(low, pltpu.roll(y, w - QK_HEAD // 2, 1), pltpu.roll(y, QK_HEAD // 2, 1))
        return y * cos + rot * sin

    q = qk_norm_rope(proj(4), qg_ref[...]) * (QK_HEAD ** -0.5 * LOG2E)
    dq_ref[...] = q.astype(BF16)
    kt = qk_norm_rope(proj(5), kg_ref[...]).T
    first = (lax.broadcasted_iota(jnp.int32, (w, tm), 0) % HEAD_W) < QK_HEAD
    k1t = jnp.where(first, kt, 0.0).astype(BF16)
    k2t = jnp.where(first, 0.0, kt).astype(BF16)
    v = proj(6).astype(BF16)
    ones_col = jnp.where(lax.broadcasted_iota(jnp.int32, (tm, HEAD_W), 1) == 0, 1.0, 0.0).astype(BF16)
    pieces = []
    for hd in range(DIFF_HEADS):
        pieces += [v[:, hd * HEAD_W:(hd + 1) * HEAD_W], ones_col]
    dk1_ref[...] = k1t
    dk2_ref[...] = k2t
    dv_ref[...] = jnp.concatenate(pieces, axis=1)


def _inproj(x2, modl, g1, w_bf, qg, kg, cos, sin, *, row0, mtpb, tpb, tm):
    r, d = x2.shape
    w = 4 * HEAD_W
    n_tiles = r // tm
    nb, t_kv = n_tiles // tpb, tpb * tm
    row = lambda i: (i, 0)
    full = lambda i: (0, 0)
    tab = lambda i: (i % tpb, 0)
    out = jax.ShapeDtypeStruct((r, w), BF16)
    kt_spec = pl.BlockSpec((None, w, tm), lambda i: (i // tpb, 0, i % tpb))
    kt_shape = jax.ShapeDtypeStruct((nb, w, t_kv), BF16)
    return pl.pallas_call(
        functools.partial(_inproj_kernel, row0=row0, mtpb=mtpb, d=d),
        grid=(n_tiles,),
        in_specs=[pl.BlockSpec((tm, d), row),
                  pl.BlockSpec(modl.shape, full),
                  pl.BlockSpec((1, d), full),
                  pl.BlockSpec(w_bf.shape, full),
                  pl.BlockSpec((1, w), full),
                  pl.BlockSpec((1, w), full),
                  pl.BlockSpec((tm, LANES), tab),
                  pl.BlockSpec((tm, LANES), tab)],
        out_specs=[pl.BlockSpec((tm, w), row)] * 5
        + [kt_spec, kt_spec, pl.BlockSpec((None, tm, 2 * w), lambda i: (i // tpb, i % tpb, 0))],
        out_shape=[out] * 5 + [kt_shape, kt_shape, jax.ShapeDtypeStruct((nb, t_kv, 2 * w), BF16)],
        compiler_params=_cparams("parallel"),
        name="inproj",
    )(x2, modl, g1, w_bf, qg, kg, cos, sin)


def _ret_kernel(lg_ref, q_ref, k_ref, v_ref, g_ref, gn_ref, s0f_ref, s0b_ref,
                o_ref, sf_ref, sb_ref, of_scr, ob_scr, *, c, n_chunks):
    hd = pl.program_id(1)
    lgf = lg_ref[0, hd]
    lgb = lg_ref[1, hd]
    rel = (lax.broadcasted_iota(jnp.int32, (c, c), 0) - lax.broadcasted_iota(jnp.int32, (c, c), 1)).astype(F32)
    dmat_f = jnp.where(rel >= 0, jnp.exp(lgf * jnp.maximum(rel, 0.0)), 0.0)
    dmat_b = jnp.where(rel <= 0, jnp.exp(lgb * jnp.maximum(-rel, 0.0)), 0.0)
    row = lax.broadcasted_iota(jnp.int32, (c, HEAD_W), 0).astype(F32)
    zeta_f = jnp.exp(lgf * (c - 1.0 - row))
    xi_f = jnp.exp(lgf * (row + 1.0))
    zeta_b = jnp.exp(lgb * row)
    xi_b = jnp.exp(lgb * (c - row))
    dec_f = jnp.exp(jnp.full((HEAD_W, HEAD_W), lgf * c, F32))
    dec_b = jnp.exp(jnp.full((HEAD_W, HEAD_W), lgb * c, F32))

    sf_ref[0, 0] = s0f_ref[0, 0]
    sb_ref[0, 0] = s0b_ref[0, 0]

    nt = (((1,), (1,)), ((), ()))

    def one(ci, dmat, zeta, xi, dec, s_ref, o_scr):
        rows = pl.ds(pl.multiple_of(ci * c, c), c)
        q = q_ref[rows, :]
        k = k_ref[rows, :]
        v = v_ref[rows, :]
        s = s_ref[0, 0]
        sc = lax.dot_general(q, k, nt, preferred_element_type=F32) * dmat
        o = jnp.dot(sc.astype(BF16), v, preferred_element_type=F32)
        o = o + xi * jnp.dot(q, s.astype(BF16), preferred_element_type=F32)
        o_scr[rows, :] = o
        kzt = (k.astype(F32) * zeta).T.astype(BF16)
        s_ref[0, 0] = dec * s + jnp.dot(kzt, v, preferred_element_type=F32)

    unroll = _pick(n_chunks, (8, 4, 2, 1))

    def body(i, carry):
        for u in range(unroll):
            ci = i * unroll + u
            one(ci, dmat_f, zeta_f, xi_f, dec_f, sf_ref, of_scr)
            one(n_chunks - 1 - ci, dmat_b, zeta_b, xi_b, dec_b, sb_ref, ob_scr)
        return carry

    lax.fori_loop(0, n_chunks // unroll, body, 0)

    gn = gn_ref[...]
    fr = unroll * c

    def fin(ci, carry):
        rows = pl.ds(pl.multiple_of(ci * fr, fr), fr)
        y = of_scr[rows, :] + ob_scr[rows, :]
        mu = jnp.sum(y, axis=-1, keepdims=True) * (1.0 / HEAD_W)
        yc = y - mu
        var = jnp.sum(yc * yc, axis=-1, keepdims=True) * (1.0 / HEAD_W)
        yn = yc * lax.rsqrt(var + EPS) * gn
        o_ref[rows, :] = (_silu(g_ref[rows, :].astype(F32)) * yn).astype(BF16)
        return carry

    lax.fori_loop(0, n_chunks // unroll, fin, 0)


def _retention(lg, rq, rk, rv, rg, gn, s0f, s0b, *, nb, t):
    c = _pick(t, (256, CHUNK))
    n_chunks = t // c
    blk = pl.BlockSpec((t, HEAD_W), lambda b, h: (b, h))
    sblk = pl.BlockSpec((1, 1, HEAD_W, HEAD_W), lambda b, h: (b, h, 0, 0))
    s_shape = jax.ShapeDtypeStruct((nb, RET_HEADS, HEAD_W, HEAD_W), F32)
    return pl.pallas_call(
        functools.partial(_ret_kernel, c=c, n_chunks=n_chunks),
        grid=(nb, RET_HEADS),
        in_specs=[pl.BlockSpec(memory_space=pltpu.SMEM),
                  blk, blk, blk, blk,
                  pl.BlockSpec((1, HEAD_W), lambda b, h: (0, h)),
                  sblk, sblk],
        out_specs=[blk, sblk, sblk],
        out_shape=[jax.ShapeDtypeStruct(rq.shape, BF16), s_shape, s_shape],
        scratch_shapes=[pltpu.VMEM((t, HEAD_W), F32), pltpu.VMEM((t, HEAD_W), F32)],
        compiler_params=_cparams("parallel", "parallel"),
        name="retention",
    )(lg, rq, rk, rv, rg, gn, s0f, s0b)


def _attn_kernel(lam_ref, q_ref, *rest, tk, post):
    kv_refs, (g_ref, o_ref, a1_scr, a2_scr) = rest[:-4], rest[-4:]

    @pl.when(pl.program_id(0) == 0)
    def _():
        a1_scr[...] = jnp.ones(a1_scr.shape, F32)
        a2_scr[...] = jnp.ones(a2_scr.shape, F32)

    def finish(a_scr):
        a = a_scr[...]
        return a[:, :HEAD_W] * (1.0 / a[:, HEAD_W:HEAD_W + 1])

    o = finish(a1_scr) - lam_ref[0] * finish(a2_scr)
    ms = jnp.sum(o * o, axis=-1, keepdims=True) * (1.0 / HEAD_W)
    o_ref[...] = (o * lax.rsqrt(ms + EPS) * g_ref[...] * post).astype(BF16)

    q = q_ref[...]

    def component(c):
        m = a = None
        for src in range(0, len(kv_refs), 3):
            kt_ref, v_ref = kv_refs[src + c], kv_refs[src + 2]
            for c0 in range(0, kt_ref.shape[1], tk):
                s = jnp.dot(q, kt_ref[:, c0:c0 + tk], preferred_element_type=F32)
                mc = jnp.max(s, axis=-1, keepdims=True)
                mn = mc if m is None else jnp.maximum(m, mc)
                p = jnp.exp2(s - mn).astype(BF16)
                pv = jnp.dot(p, v_ref[c0:c0 + tk, :], preferred_element_type=F32)
                a = pv if m is None else jnp.exp2(m - mn) * a + pv
                m = mn
        return a

    a1_scr[...] = component(0)
    a2_scr[...] = component(1)


def _attention(lam, q, kv_sets, g, *, nb, tq_total, post):
    tq = _pick(tq_total, (512, 256, 128))
    nq = tq_total // tq
    tk = _pick(math.gcd(*[s[0].shape[2] for s in kv_sets]), (256, 128))
    n = nb * DIFF_HEADS * nq

    def tile(t):
        return t // (DIFF_HEADS * nq), (t // nq) % DIFF_HEADS, t % nq

    def at(f, lag):
        return lambda s: f(*tile(jnp.maximum(s - 1, 0) if lag else jnp.minimum(s, n - 1)))

    qmap = lambda b, h, i: (b * nq + i, h)
    kv_specs, kv_args = [], []
    for k1t, k2t, vx in kv_sets:
        t_src = k1t.shape[2]
        kt_spec = pl.BlockSpec((None, HEAD_W, t_src), at(lambda b, h, i: (b, h, 0), False))
        kv_specs += [kt_spec, kt_spec, pl.BlockSpec((None, t_src, 2 * HEAD_W), at(lambda b, h, i: (b, 0, h), False))]
        kv_args += [k1t, k2t, vx]
    return pl.pallas_call(
        functools.partial(_attn_kernel, tk=tk, post=post),
        grid=(n + 1,),
        in_specs=[pl.BlockSpec(memory_space=pltpu.SMEM), pl.BlockSpec((tq, HEAD_W), at(qmap, False))]
        + kv_specs + [pl.BlockSpec((1, HEAD_W), lambda s: (0, 0))],
        out_specs=pl.BlockSpec((tq, HEAD_W), at(qmap, True)),
        out_shape=jax.ShapeDtypeStruct(q.shape, BF16),
        scratch_shapes=[pltpu.VMEM((tq, 2 * HEAD_W), F32), pltpu.VMEM((tq, 2 * HEAD_W), F32)],
        compiler_params=_cparams("arbitrary"),
        name="attention",
    )(lam, q, *kv_args, g)


def _mixed_residual(ret_ref, dif_ref, w_ref, x_ref, gate):
    half = w_ref.shape[0] // 2
    mix = (jnp.dot(ret_ref[...], w_ref[:half, :], preferred_element_type=F32)
           + jnp.dot(dif_ref[...], w_ref[half:, :], preferred_element_type=F32))
    return x_ref[...] + gate * mix


def _swiglu_tile(h, wg_ref, wu_ref, wd_ref, fc):
    ff = wg_ref.shape[-1]
    acc = None
    for c0 in range(0, ff, fc):
        g = jnp.dot(h, wg_ref[0, :, c0:c0 + fc].astype(BF16), preferred_element_type=F32)
        u = jnp.dot(h, wu_ref[0, :, c0:c0 + fc].astype(BF16), preferred_element_type=F32)
        a = (_silu(g) * u).astype(BF16)
        part = jnp.dot(a, wd_ref[0, c0:c0 + fc, :].astype(BF16), preferred_element_type=F32)
        acc = part if acc is None else acc + part
    return acc


def _mix_ffn_kernel(ret_ref, dif_ref, wo_ref, x_ref, mod_ref, g_ref, wg_ref, wu_ref, wd_ref, o_ref,
                    *, row0, tpb, d, fc):
    r = row0 + pl.program_id(0) // tpb
    gate1, shift, scale, gate2 = _mod_rows(mod_ref, r, d, (2, 3, 4, 5))
    x = _mixed_residual(ret_ref, dif_ref, wo_ref, x_ref, gate1)
    h = _norm_mod(x, g_ref[...], shift, scale).astype(BF16)
    o_ref[...] = x + gate2 * _swiglu_tile(h, wg_ref, wu_ref, wd_ref, fc)


def _ff_chunk(ff):
    return _pick(ff, (256, 128))


def _mix_ffn(ret, dif, wo_bf, x2, modl, g2, wg, wu, wd, *, row0, tpb, tm):
    r, d = x2.shape
    ff = wg.shape[-1]
    row = lambda i: (i, 0)
    full = lambda i: (0, 0)
    wfull = lambda i: (0, 0, 0)
    return pl.pallas_call(
        functools.partial(_mix_ffn_kernel, row0=row0, tpb=tpb, d=d, fc=_ff_chunk(ff)),
        grid=(r // tm,),
        in_specs=[pl.BlockSpec((tm, ret.shape[1]), row),
                  pl.BlockSpec((tm, dif.shape[1]), row),
                  pl.BlockSpec(wo_bf.shape, full),
                  pl.BlockSpec((tm, d), row),
                  pl.BlockSpec(modl.shape, full),
                  pl.BlockSpec((1, d), full),
                  pl.BlockSpec((1, d, ff), wfull, pipeline_mode=pl.Buffered(1)),
                  pl.BlockSpec((1, d, ff), wfull, pipeline_mode=pl.Buffered(1)),
                  pl.BlockSpec((1, ff, d), wfull, pipeline_mode=pl.Buffered(1))],
        out_specs=pl.BlockSpec((tm, d), row),
        out_shape=jax.ShapeDtypeStruct((r, d), F32),
        compiler_params=_cparams("parallel"),
        name="mix_ffn",
    )(ret, dif, wo_bf, x2, modl, g2, wg, wu, wd)


def _rows_to_wide(ref, tm, lead=()):
    return jnp.concatenate([ref[lead + (pl.ds(k, tm, stride=SUBLANES), slice(None))] for k in range(SUBLANES)],
                           axis=1)


def _wide_to_rows(ref, val):
    tm = val.shape[0]
    for k in range(SUBLANES):
        ref[pl.ds(k, tm, stride=SUBLANES), :] = val[:, k * LANES:(k + 1) * LANES]


def _row_tile(ref, r):
    return ref.at[pl.ds(pl.multiple_of(r * SUBLANES, SUBLANES), SUBLANES), :]


ROW_DMA_UNROLL = 16


def _start_rows(n, copy):
    def body(i, carry):
        for u in range(ROW_DMA_UNROLL):
            copy(i * ROW_DMA_UNROLL + u).start(priority=u % 2)
        return carry
    lax.fori_loop(0, n // ROW_DMA_UNROLL, body, 0)


def _wait_rows(n, copy):
    def body(i, carry):
        for u in range(ROW_DMA_UNROLL):
            copy(i * ROW_DMA_UNROLL + u).wait()
        return carry
    lax.fori_loop(0, n // ROW_DMA_UNROLL, body, 0)


def _ffn_group_kernel(te_ref, nu_ref, h_ref, wg_ref, wu_ref, wd_ref, o_ref, *, tm, fc):
    i = pl.program_id(0)

    @pl.when(i < nu_ref[0])
    def _():
        _wide_to_rows(o_ref, _swiglu_tile(_rows_to_wide(h_ref, tm).astype(BF16), wg_ref, wu_ref, wd_ref, fc))

    @pl.when(i >= nu_ref[0])
    def _():
        o_ref[...] = jnp.zeros(o_ref.shape, F32)


def _ffn_grouped(tile_expert, n_used, hs, wg, wu, wd, *, tm):
    d, ff = wg.shape[-2:]
    assert d == SUBLANES * LANES
    rows = hs.shape[0] // SUBLANES
    row = lambda i, te, nu: (i, 0)
    wsel = lambda i, te, nu: (te[i], 0, 0)
    return pl.pallas_call(
        functools.partial(_ffn_group_kernel, tm=tm, fc=_ff_chunk(ff)),
        grid_spec=pltpu.PrefetchScalarGridSpec(
            num_scalar_prefetch=2,
            grid=(rows // tm,),
            in_specs=[pl.BlockSpec((tm * SUBLANES, LANES), row),
                      pl.BlockSpec((1, d, ff), wsel),
                      pl.BlockSpec((1, d, ff), wsel),
                      pl.BlockSpec((1, ff, d), wsel)],
            out_specs=pl.BlockSpec((tm * SUBLANES, LANES), row)),
        out_shape=jax.ShapeDtypeStruct(hs.shape, F32),
        compiler_params=_cparams("arbitrary"),
        name="ffn_grouped",
    )(tile_expert, n_used, hs, wg, wu, wd)


def _mix_route_kernel(ret_ref, dif_ref, wo_ref, x_ref, mod_ref, g_ref, rw_ref, rb_ref,
                      xo_ref, ri_ref, rwt_ref, cnt_ref, carry, *, tpb, d, ne):
    i = pl.program_id(0)

    @pl.when(i == 0)
    def _():
        carry[...] = jnp.zeros(carry.shape, F32)

    gate, shift, scale = _mod_rows(mod_ref, i // tpb, d, (2, 3, 4))
    x = _mixed_residual(ret_ref, dif_ref, wo_ref, x_ref, gate)
    xo_ref[...] = x
    h = _norm_mod(x, g_ref[...], shift, scale)
    tm = h.shape[0]
    nt = (((1,), (1,)), ((), ()))
    logits = lax.dot_general(rw_ref[...], h, nt, preferred_element_type=F32, precision=HIGHEST)
    logits = logits + rb_ref[...]
    eid = lax.broadcasted_iota(jnp.int32, (ne, tm), 0)
    m1 = jnp.max(logits, axis=0, keepdims=True)
    i1 = jnp.min(jnp.where(logits == m1, eid, ne), axis=0, keepdims=True)
    rest = jnp.where(eid == i1, -jnp.inf, logits)
    m2 = jnp.max(rest, axis=0, keepdims=True)
    i2 = jnp.min(jnp.where(rest == m2, eid, ne), axis=0, keepdims=True)
    e2 = jnp.exp(m2 - m1)
    w1 = 1.0 / (1.0 + e2)
    w2 = e2 * w1
    sel1 = eid == i1
    sel2 = eid == i2
    member = jnp.where(sel1, 1.0, jnp.where(sel2, 1.0, 0.0))
    tr = lax.broadcasted_iota(jnp.int32, (tm, tm), 0)
    tc = lax.broadcasted_iota(jnp.int32, (tm, tm), 1)
    before = jnp.where(tr < tc, 1.0, 0.0).astype(BF16)
    prefix = jnp.dot(member.astype(BF16), before, preferred_element_type=F32) + carry[:, 0:1]
    rank1 = jnp.sum(jnp.where(sel1, prefix, 0.0), axis=0, keepdims=True)
    rank2 = jnp.sum(jnp.where(sel2, prefix, 0.0), axis=0, keepdims=True)
    zi = jnp.zeros((SUBLANES - 4, tm), jnp.int32)
    ri_ref[...] = jnp.concatenate([i1, i2, rank1.astype(jnp.int32), rank2.astype(jnp.int32), zi], axis=0)
    rwt_ref[...] = jnp.concatenate([w1, w2, jnp.zeros((SUBLANES - 2, tm), F32)], axis=0)
    carry[...] = carry[...] + jnp.sum(member, axis=1, keepdims=True)
    cnt_ref[...] = carry[...]


def _mix_route(ret, dif, wo_bf, x2, modl, g2, rw_t, rb, *, tpb, tm):
    n, d = x2.shape
    ne = rw_t.shape[0]
    row = lambda i: (i, 0)
    full = lambda i: (0, 0)
    return pl.pallas_call(
        functools.partial(_mix_route_kernel, tpb=tpb, d=d, ne=ne),
        grid=(n // tm,),
        in_specs=[pl.BlockSpec((tm, ret.shape[1]), row),
                  pl.BlockSpec((tm, dif.shape[1]), row),
                  pl.BlockSpec(wo_bf.shape, full),
                  pl.BlockSpec((tm, d), row),
                  pl.BlockSpec(modl.shape, full),
                  pl.BlockSpec((1, d), full),
                  pl.BlockSpec((ne, d), full),
                  pl.BlockSpec((ne, 1), full)],
        out_specs=[pl.BlockSpec((tm, d), row),
                   pl.BlockSpec((SUBLANES, tm), lambda i: (0, i)),
                   pl.BlockSpec((SUBLANES, tm), lambda i: (0, i)),
                   pl.BlockSpec((ne, LANES), full)],
        out_shape=[jax.ShapeDtypeStruct((n, d), F32),
                   jax.ShapeDtypeStruct((SUBLANES, n), jnp.int32),
                   jax.ShapeDtypeStruct((SUBLANES, n), F32),
                   jax.ShapeDtypeStruct((ne, LANES), F32)],
        scratch_shapes=[pltpu.VMEM((ne, LANES), F32)],
        compiler_params=_cparams("arbitrary"),
        name="mix_route",
    )(ret, dif, wo_bf, x2, modl, g2, rw_t, rb)


def _pos_kernel(off_ref, ri_ref, pos_ref, *, ne):
    ri = ri_ref[...]
    e = ri[0:2, :]
    pos = ri[2:4, :]
    for k in range(ne):
        pos = pos + jnp.where(e == k, off_ref[k], 0)
    pos_ref[...] = pos


def _positions(offsets, route_i, *, ne):
    n = route_i.shape[1]
    return pl.pallas_call(
        functools.partial(_pos_kernel, ne=ne),
        in_specs=[pl.BlockSpec(memory_space=pltpu.SMEM),
                  pl.BlockSpec(route_i.shape, lambda: (0, 0))],
        out_specs=pl.BlockSpec((TOP_K, n), lambda: (0, 0)),
        out_shape=jax.ShapeDtypeStruct((TOP_K, n), jnp.int32),
        name="positions",
    )(offsets, route_i)


def _dispatch_kernel(pos_ref, pad_ref, x_ref, mod_ref, g_ref, xs_ref, h_scr, z_scr, sem, zsem, *, tpb, d, n):
    i = pl.program_id(0)
    tm = x_ref.shape[0]
    n_pad = pad_ref.shape[0]
    shift, scale = _mod_rows(mod_ref, i // tpb, d, (3, 4))
    _wide_to_rows(h_scr, _norm_mod(x_ref[...], g_ref[...], shift, scale))

    def copy(j, t):
        dst = pos_ref[j * n + i * tm + t]
        return pltpu.make_async_copy(_row_tile(h_scr, t), _row_tile(xs_ref, dst), sem)

    def zero(k):
        return pltpu.make_async_copy(z_scr, _row_tile(xs_ref, pad_ref[k]), zsem)

    @pl.when(i == 0)
    def _():
        z_scr[...] = jnp.zeros(z_scr.shape, F32)
        _start_rows(n_pad, zero)

    for j in range(TOP_K):
        _start_rows(tm, functools.partial(copy, j))

    @pl.when(i == 0)
    def _():
        _wait_rows(n_pad, zero)

    for j in range(TOP_K):
        _wait_rows(tm, functools.partial(copy, j))


def _dispatch(pos_flat, pad_rows, x2, modl, g2, *, rows, tpb, tm):
    n, d = x2.shape
    assert rows == TOP_K * n + pad_rows.shape[0] and pad_rows.shape[0] % ROW_DMA_UNROLL == 0
    return pl.pallas_call(
        functools.partial(_dispatch_kernel, tpb=tpb, d=d, n=n),
        grid_spec=pltpu.PrefetchScalarGridSpec(
            num_scalar_prefetch=2,
            grid=(n // tm,),
            in_specs=[pl.BlockSpec((tm, d), lambda i, pos, pad: (i, 0)),
                      pl.BlockSpec(modl.shape, lambda i, pos, pad: (0, 0)),
                      pl.BlockSpec((1, d), lambda i, pos, pad: (0, 0))],
            out_specs=pl.BlockSpec(memory_space=pl.ANY),
            scratch_shapes=[pltpu.VMEM((tm * SUBLANES, LANES), F32), pltpu.VMEM((SUBLANES, LANES), F32),
                            pltpu.SemaphoreType.DMA(()), pltpu.SemaphoreType.DMA(())]),
        out_shape=jax.ShapeDtypeStruct((rows * SUBLANES, LANES), F32),
        compiler_params=pltpu.CompilerParams(dimension_semantics=("arbitrary",), vmem_limit_bytes=VMEM_LIMIT,
                                             has_side_effects=True),
        name="dispatch",
    )(pos_flat, pad_rows, x2, modl, g2)


def _combine_kernel(pos_ref, ys_ref, w_ref, x_ref, mod_ref, o_ref, ybuf, sem, *, tpb, d, n, n_steps):
    i = pl.program_id(0)
    tm = x_ref.shape[0]

    def copy(step, slot, j, t):
        src = pos_ref[j * n + step * tm + t]
        return pltpu.make_async_copy(_row_tile(ys_ref, src), _row_tile(ybuf.at[slot, j], t), sem.at[slot])

    def fetch(step, slot):
        for j in range(TOP_K):
            _start_rows(tm, functools.partial(copy, step, slot, j))

    @pl.when(i == 0)
    def _():
        fetch(0, 0)

    @pl.when(i + 1 < n_steps)
    def _():
        fetch(i + 1, (i + 1) % 2)

    slot = i % 2
    for j in range(TOP_K):
        _wait_rows(tm, functools.partial(copy, i, slot, j))

    (gate,) = _mod_rows(mod_ref, i // tpb, d, (5,))
    w = w_ref[...]
    y = w[:, 0:1] * _rows_to_wide(ybuf, tm, (slot, 0)) + w[:, 1:2] * _rows_to_wide(ybuf, tm, (slot, 1))
    o_ref[...] = x_ref[...] + gate * y


def _combine(pos_flat, ys, w_t, x2, modl, *, tpb, tm):
    n, d = x2.shape
    n_steps = n // tm
    row = lambda i, pos: (i, 0)
    return pl.pallas_call(
        functools.partial(_combine_kernel, tpb=tpb, d=d, n=n, n_steps=n_steps),
        grid_spec=pltpu.PrefetchScalarGridSpec(
            num_scalar_prefetch=1,
            grid=(n_steps,),
            in_specs=[pl.BlockSpec(memory_space=pl.ANY),
                      pl.BlockSpec((tm, TOP_K), row),
                      pl.BlockSpec((tm, d), row),
                      pl.BlockSpec(modl.shape, lambda i, pos: (0, 0))],
            out_specs=pl.BlockSpec((tm, d), row),
            scratch_shapes=[pltpu.VMEM((2, TOP_K, tm * SUBLANES, LANES), F32), pltpu.SemaphoreType.DMA((2,))]),
        out_shape=jax.ShapeDtypeStruct((n, d), F32),
        compiler_params=pltpu.CompilerParams(dimension_semantics=("arbitrary",), vmem_limit_bytes=VMEM_LIMIT),
        name="combine",
    )(pos_flat, ys, w_t, x2, modl)


def _rope_tables(t):
    n_freq = QK_HEAD // 4
    rows = t // GRID_W
    row_pos = jnp.repeat(jnp.arange(rows, dtype=F32), GRID_W)
    col_pos = jnp.tile(jnp.arange(GRID_W, dtype=F32), rows)
    inv_freq = ROPE_BASE ** (-jnp.arange(n_freq, dtype=F32) / n_freq)
    ang = jnp.concatenate([row_pos[:, None] * inv_freq, col_pos[:, None] * inv_freq], axis=-1)
    ang = jnp.concatenate([ang, ang], axis=-1)
    sign = jnp.where(jnp.arange(QK_HEAD) < QK_HEAD // 2, -1.0, 1.0).astype(F32)
    rep = LANES // QK_HEAD
    return jnp.tile(jnp.cos(ang), (1, rep)), jnp.tile(jnp.sin(ang) * sign, (1, rep))


def _moe_plan(counts, n_routed, n_tiles, tm):
    tiles = (counts + tm - 1) // tm
    ends = jnp.cumsum(tiles)
    offsets = (ends - tiles) * tm
    n_used = ends[-1]
    ids = jnp.arange(n_tiles, dtype=jnp.int32)
    te = jnp.sum(jnp.minimum(ids, n_used - 1)[:, None] >= ends[None, :], axis=1).astype(jnp.int32)
    seg_end = jnp.concatenate([offsets[1:], jnp.full((1,), n_tiles * tm, offsets.dtype)])
    free = seg_end - (offsets + counts)
    free_end = jnp.cumsum(free)
    k = jnp.arange(n_tiles * tm - n_routed, dtype=jnp.int32)
    in_run = (k[:, None] >= (free_end - free)[None, :]) & (k[:, None] < free_end[None, :])
    pad_rows = jnp.sum(jnp.where(in_run, (offsets + counts - (free_end - free))[None, :], 0), axis=1) + k
    return offsets.astype(jnp.int32), te, n_used.reshape(1).astype(jnp.int32), pad_rows.astype(jnp.int32)


def kernel(x, c, ctx, c_ctx, ada_w, ada_b, norm1_g, norm2_g, w_in, ret_decay_fwd, ret_decay_bwd, ret_gn_g, diff_qn_g, diff_kn_g, lam_q1, lam_k1, lam_q2, lam_k2, diff_subln_g, w_out, ffn_w_gate, ffn_w_up, ffn_w_down, router_w, router_b, moe_w_gate, moe_w_up, moe_w_down):
    b, t, d = x.shape
    tc = ctx.shape[1]
    depth = ada_w.shape[0]
    ne = router_w.shape[-1]
    w = 4 * HEAD_W
    assert b + 1 <= SUBLANES and t % GRID_W == 0 and t % CHUNK == 0 and tc % CHUNK == 0
    assert w_in.shape[-1] == 7 * w and d % LANES == 0

    c8 = jnp.zeros((SUBLANES, d), F32).at[:b].set(c).at[b].set(c_ctx)
    mods = _ada(c8, ada_w, ada_b)

    cos, sin = _rope_tables(t)
    cos_c = jnp.ones((b * tc, LANES), F32)
    sin_c = jnp.zeros((b * tc, LANES), F32)
    tm = _pick(t, (512, 256, 128))
    tmc = _pick(tc, (256, 128))
    tpb = t // tm
    ntc = (b * tc) // tmc
    assert t % tmc == 0
    rep4 = lambda g: jnp.tile(g.astype(F32), w // g.shape[0]).reshape(1, w)

    xl = x.reshape(b * t, d)
    xc = ctx.reshape(b * tc, d)
    zero_state = jnp.zeros((b, RET_HEADS, HEAD_W, HEAD_W), F32)

    for l in range(depth):
        ctx_out = l < depth - 1
        lam_init = 0.8 - 0.6 * math.exp(-0.3 * l)
        modl = mods[l]
        g1 = norm1_g[l].reshape(1, d)
        g2 = norm2_g[l].reshape(1, d)
        w_bf = w_in[l].astype(BF16)
        wo_bf = w_out[l].astype(BF16)
        qg, kg = rep4(diff_qn_g[l]), rep4(diff_kn_g[l])
        gn = ret_gn_g[l].reshape(1, w)
        sg = diff_subln_g[l].reshape(1, HEAD_W)
        lg = jnp.stack([jax.nn.log_sigmoid(ret_decay_fwd[l].astype(F32)),
                        jax.nn.log_sigmoid(ret_decay_bwd[l].astype(F32))])
        lam = (jnp.exp(jnp.sum(lam_q1[l].astype(F32) * lam_k1[l].astype(F32)))
               - jnp.exp(jnp.sum(lam_q2[l].astype(F32) * lam_k2[l].astype(F32))) + lam_init).reshape(1)
        post = 1.0 - lam_init

        rq, rk, rv, rg, dq, *kv_l = _inproj(
            xl, modl, g1, w_bf, qg, kg, cos, sin, row0=0, mtpb=tpb, tpb=tpb, tm=tm)
        rqc, rkc, rvc, rgc, dqc, *kv_c = _inproj(
            xc, modl, g1, w_bf, qg, kg, cos_c, sin_c, row0=b, mtpb=ntc, tpb=tc // tmc, tm=tmc)

        ret_c, sc_f, sc_b = _retention(lg, rqc, rkc, rvc, rgc, gn, zero_state, zero_state, nb=b, t=tc)
        ret, _, _ = _retention(lg, rq, rk, rv, rg, gn, sc_f, sc_b, nb=b, t=t)

        dif = _attention(lam, dq, (kv_l, kv_c), sg, nb=b, tq_total=t, post=post)
        if ctx_out:
            dif_c = _attention(lam, dqc, (kv_c,), sg, nb=b, tq_total=tc, post=post)

        if l % 2 == 0:
            j = l // 2
            wg, wu, wd = ffn_w_gate[j:j + 1], ffn_w_up[j:j + 1], ffn_w_down[j:j + 1]
            xl = _mix_ffn(ret, dif, wo_bf, xl, modl, g2, wg, wu, wd, row0=0, tpb=tpb, tm=tm)
            if ctx_out:
                xc = _mix_ffn(ret_c, dif_c, wo_bf, xc, modl, g2, wg, wu, wd, row0=b, tpb=ntc, tm=tmc)
        else:
            if ctx_out:
                raise NotImplementedError("routed channel mixer on context tokens")
            j = l // 2
            n = b * t
            tme = _pick(t, (512, 256, 128))
            n_tiles = (n * TOP_K) // tme + ne
            xl, route_i, route_w, counts = _mix_route(
                ret, dif, wo_bf, xl, modl, g2, router_w[j].T.astype(F32),
                router_b[j].reshape(ne, 1).astype(F32), tpb=tpb, tm=tm)
            offsets, tile_expert, n_used, pad_rows = _moe_plan(
                counts[:, 0].astype(jnp.int32), TOP_K * n, n_tiles, tme)
            pos = _positions(offsets, route_i, ne=ne)
            pos_flat = pos.reshape(TOP_K * n)
            xs = _dispatch(pos_flat, pad_rows, xl, modl, g2, rows=n_tiles * tme, tpb=t // tme, tm=tme)
            ys = _ffn_grouped(tile_expert, n_used, xs,
                              moe_w_gate[j].astype(BF16), moe_w_up[j].astype(BF16),
                              moe_w_down[j].astype(BF16), tm=tme)
            xl = _combine(pos_flat, ys, route_w[:TOP_K].T, xl, modl, tpb=t // tme, tm=tme)
    return xl.reshape(b, t, d)
```
